```python
import math
import jax
import jax.numpy as jnp
from jax import lax
import numpy as np

D_MODEL = 4096
BATCH = 2
SEQ = 8192
DEPTH = 2

CTX_LEN = 256
GRID_W = 64
CHUNK = 64

H_RET = 8
RET_W = D_MODEL // 2
HD_RET = RET_W // H_RET
H_ML = 8
ML_W = D_MODEL // 2
HD_ML = ML_W // H_ML
AB_MIX_W = RET_W + ML_W
AB_IN_W = 4 * RET_W + 4 * ML_W + 4 * H_ML
ROPE_BASE = 10000.0

HD_DN = 128
DN_W = D_MODEL
H_DN = DN_W // HD_DN
CONV_K = 3
DN_IN_W = 4 * DN_W + 4 * H_DN

N_EXPERTS = 64
EXPERT_HIDDEN = 256
SHARED_HIDDEN = 256
TOP_K = 8
N_GROUPS = 8
TOPK_GROUPS = 4
ROUTED_SCALE = 2.5

LN_EPS = 1e-5
N_EVEN = (DEPTH + 1) // 2
N_ODD = DEPTH // 2
DEEPNORM_ALPHA = (2 * DEPTH) ** 0.25
DEEPNORM_BETA = (8 * DEPTH) ** -0.25
F32 = jnp.float32

kernel_name = 'hybrid_ret_mlstm_gdn_moe_diffusion'


def layer_norm(x, g, b):
    xf = x.astype(F32)
    mu = jnp.mean(xf, -1, keepdims=True)
    var = jnp.mean(jnp.square(xf - mu), -1, keepdims=True)
    y = (xf - mu) * lax.rsqrt(var + LN_EPS) * g.astype(F32) + b.astype(F32)
    return y.astype(x.dtype)


def head_layernorm(t, g):
    mu = jnp.mean(t, -1, keepdims=True)
    var = jnp.mean(jnp.square(t - mu), -1, keepdims=True)
    return (t - mu) * lax.rsqrt(var + LN_EPS) * g.astype(F32).reshape(t.shape[-2], t.shape[-1])


def head_rmsnorm(t, g):
    return t * lax.rsqrt(jnp.mean(jnp.square(t), -1, keepdims=True) + 1e-6) * g.astype(F32)


def l2norm(t):
    t = t.astype(F32)
    return t * lax.rsqrt(jnp.sum(jnp.square(t), -1, keepdims=True) + 1e-6)


def split_heads(t, n_heads):
    return t.reshape(t.shape[0], t.shape[1], n_heads, t.shape[2] // n_heads)


def rope(t, pos):
    half = t.shape[-1] // 2
    inv_freq = ROPE_BASE ** (-jnp.arange(half, dtype=F32) / half)
    ang = pos[:, None] * inv_freq[None, :]
    cos = jnp.cos(ang)[None, :, None, :]
    sin = jnp.sin(ang)[None, :, None, :]
    t = t.astype(F32)
    t1, t2 = t[..., :half], t[..., half:]
    return jnp.concatenate([t1 * cos - t2 * sin, t1 * sin + t2 * cos], axis=-1)


def to_chunks(t):
    b, n, h = t.shape[:3]
    t = t.reshape(b, n // CHUNK, CHUNK, h, *t.shape[3:])
    return jnp.moveaxis(t, (1, 3), (0, 2))


def from_chunks(t):
    t = jnp.moveaxis(t, (0, 2), (1, 3))
    return t.reshape(t.shape[0], t.shape[1] * t.shape[2], t.shape[3], t.shape[4])


def retention_scan(q, k, v, log_gamma, state0):
    q, k, v = q.astype(F32), k.astype(F32), v.astype(F32)
    pos = jnp.arange(CHUNK, dtype=F32)
    rel = pos[:, None] - pos[None, :]
    intra = jnp.where(rel >= 0, jnp.exp(log_gamma[:, None, None] * jnp.maximum(rel, 0.0)), 0.0)
    q_dec = jnp.exp(log_gamma[:, None] * (pos + 1.0))[:, :, None]
    k_dec = jnp.exp(log_gamma[:, None] * (CHUNK - 1.0 - pos))[:, :, None]
    c_dec = jnp.exp(log_gamma * CHUNK)[:, None, None]

    def step(s, xs):
        qc, kc, vc = xs
        att = jnp.einsum('bhid,bhjd->bhij', qc, kc) * intra
        o = jnp.einsum('bhij,bhjv->bhiv', att, vc) + jnp.einsum('bhid,bhdv->bhiv', qc * q_dec, s)
        s = s * c_dec + jnp.einsum('bhjd,bhjv->bhdv', kc * k_dec, vc)
        return s, o

    s, o = lax.scan(step, state0, (to_chunks(q), to_chunks(k), to_chunks(v)))
    return from_chunks(o), s


def mlstm_scan(q, k, v, i_pre, f_pre, state0):
    q, k, v = q.astype(F32), k.astype(F32), v.astype(F32)
    causal = jnp.tril(jnp.ones((CHUNK, CHUNK), dtype=bool))

    def step(carry, xs):
        c_mat, n_vec, m = carry
        qc, kc, vc, ic, fc = xs
        b = jnp.cumsum(jax.nn.log_sigmoid(fc), axis=-1)
        d_log = jnp.where(causal, b[..., :, None] - b[..., None, :] + ic[..., None, :], -jnp.inf)
        inter_log = b + m[..., None]
        m_row = jnp.maximum(inter_log, jnp.max(d_log, -1))
        d_w = jnp.exp(d_log - m_row[..., None])
        inter_w = jnp.exp(inter_log - m_row)
        s = jnp.einsum('bhid,bhjd->bhij', qc, kc) * d_w
        num = jnp.einsum('bhij,bhjv->bhiv', s, vc) + inter_w[..., None] * jnp.einsum('bhid,bhdv->bhiv', qc, c_mat)
        den = jnp.sum(s, -1) + inter_w * jnp.einsum('bhid,bhd->bhi', qc, n_vec)
        h = num / jnp.maximum(jnp.abs(den), jnp.exp(-m_row))[..., None]
        w_last = d_w[..., -1, :]
        dec_last = inter_w[..., -1]
        c_mat = dec_last[..., None, None] * c_mat + jnp.einsum('bhjd,bhjv->bhdv', kc * w_last[..., None], vc)
        n_vec = dec_last[..., None] * n_vec + jnp.einsum('bhj,bhjd->bhd', w_last, kc)
        return (c_mat, n_vec, m_row[..., -1]), h

    xs = (to_chunks(q), to_chunks(k), to_chunks(v), to_chunks(i_pre.astype(F32)), to_chunks(f_pre.astype(F32)))
    state, h = lax.scan(step, state0, xs)
    return from_chunks(h), state


def gated_delta_scan(q, k, v, g, beta, state0):
    q, k, v = q.astype(F32), k.astype(F32), v.astype(F32)
    incl = jnp.tril(jnp.ones((CHUNK, CHUNK), dtype=bool))
    strict = jnp.tril(jnp.ones((CHUNK, CHUNK), dtype=bool), -1)
    eye = jnp.eye(CHUNK, dtype=F32)
    dv = v.shape[-1]

    def step(s, xs):
        qc, kc, vc, gc, bc = xs
        gcum = jnp.cumsum(gc, axis=-1)
        decay = jnp.exp(jnp.where(incl, gcum[..., :, None] - gcum[..., None, :], -jnp.inf))
        kb = kc * bc[..., None]
        a_mat = jnp.where(strict, jnp.einsum('bhid,bhjd->bhij', kb, kc) * decay, 0.0)
        rhs = jnp.concatenate([vc * bc[..., None], kb * jnp.exp(gcum)[..., None]], axis=-1)
        sol = lax.linalg.triangular_solve(eye + a_mat, rhs, left_side=True, lower=True)
        u, w = sol[..., :dv], sol[..., dv:]
        v_new = u - jnp.einsum('bhid,bhdv->bhiv', w, s)
        att = jnp.einsum('bhid,bhjd->bhij', qc, kc) * decay
        o = jnp.einsum('bhid,bhdv->bhiv', qc * jnp.exp(gcum)[..., None], s) + jnp.einsum('bhij,bhjv->bhiv', att, v_new)
        g_last = gcum[..., -1:]
        s = jnp.exp(g_last)[..., None] * s + jnp.einsum('bhjd,bhjv->bhdv', kc * jnp.exp(g_last - gcum)[..., None], v_new)
        return s, o

    xs = (to_chunks(q), to_chunks(k), to_chunks(v), to_chunks(g.astype(F32)), to_chunks(beta.astype(F32)))
    s, o = lax.scan(step, state0, xs)
    return from_chunks(o), s


def bidirectional(scan_dir, ctx_seq, lat_seq, state0):
    outs = []
    for direction in range(2):
        rev = (lambda t: jnp.flip(t, axis=1)) if direction else (lambda t: t)
        o_c, s_c = scan_dir(direction, [rev(t) for t in ctx_seq], state0)
        o_l, _ = scan_dir(direction, [rev(t) for t in lat_seq], s_c)
        outs.append((rev(o_c), rev(o_l)))
    return outs[0][0] + outs[1][0], outs[0][1] + outs[1][1]


def retention_mlstm_mixer(u_lat, u_ctx, w_in, w_out, ret_decay_rate, ret_norm_g, ml_gate_b, ml_norm_g, need_ctx):
    batch, n_lat, _ = u_lat.shape
    n_ctx = u_ctx.shape[1]
    bounds = [RET_W, 2 * RET_W, 3 * RET_W, 4 * RET_W,
              4 * RET_W + ML_W, 4 * RET_W + 2 * ML_W, 4 * RET_W + 3 * ML_W, 4 * RET_W + 4 * ML_W]

    def project(u, pos):
        length = u.shape[1]
        rq, rk, rv, rg, mq, mk, mv, mo, gates = jnp.split(u @ w_in, bounds, axis=-1)
        rq = rope(split_heads(rq, H_RET), pos)
        rk = rope(split_heads(rk, H_RET), pos) * HD_RET ** -0.5
        gates = gates.reshape(batch, length, 2, 2, H_ML).astype(F32) + ml_gate_b.astype(F32)
        ret_seq = [rq, rk, split_heads(rv, H_RET)]
        ml_seq = [split_heads(mq, H_ML), split_heads(mk, H_ML) * HD_ML ** -0.5, split_heads(mv, H_ML),
                  gates[:, :, :, 0], gates[:, :, :, 1]]
        return ret_seq, ml_seq, rg, mo

    ret_c, ml_c, rg_c, mo_c = project(u_ctx, jnp.arange(n_ctx, dtype=F32))
    ret_l, ml_l, rg_l, mo_l = project(u_lat, n_ctx + jnp.arange(n_lat, dtype=F32))

    log_gamma = -jnp.exp(ret_decay_rate.astype(F32))
    s0 = jnp.zeros((batch, H_RET, HD_RET, HD_RET), F32)
    r_ctx, r_lat = bidirectional(lambda d, s, st: retention_scan(s[0], s[1], s[2], log_gamma[d], st), ret_c, ret_l, s0)
    m0 = (jnp.zeros((batch, H_ML, HD_ML, HD_ML), F32), jnp.zeros((batch, H_ML, HD_ML), F32), jnp.zeros((batch, H_ML), F32))
    m_ctx, m_lat = bidirectional(
        lambda d, s, st: mlstm_scan(s[0], s[1], s[2], s[3][:, :, d], s[4][:, :, d], st), ml_c, ml_l, m0)

    def merge(r, m, rg, mo):
        length = r.shape[1]
        r = head_layernorm(r, ret_norm_g) * jax.nn.silu(split_heads(rg, H_RET).astype(F32))
        m = head_layernorm(m, ml_norm_g) * jax.nn.sigmoid(split_heads(mo, H_ML).astype(F32))
        y = jnp.concatenate([r.reshape(batch, length, RET_W), m.reshape(batch, length, ML_W)], axis=-1)
        return y.astype(u_lat.dtype) @ w_out

    y_lat = merge(r_lat, m_lat, rg_l, mo_l)
    y_ctx = merge(r_ctx, m_ctx, rg_c, mo_c) if need_ctx else None
    return y_lat, y_ctx


def gated_deltanet_mixer(u_lat, u_ctx, w_in, conv_w, a_log, dt_bias, norm_g, w_out, need_ctx):
    batch, n_lat, _ = u_lat.shape
    n_ctx = u_ctx.shape[1]
    rows = n_lat // GRID_W
    bounds = [3 * DN_W, 4 * DN_W, 4 * DN_W + 2 * H_DN]

    def project(u, grid_h, grid_w):
        length = u.shape[1]
        qkv, z, a, b = jnp.split(u @ w_in, bounds, axis=-1)
        img = qkv.reshape(batch, grid_h, grid_w, 3 * DN_W)
        img = lax.conv_general_dilated(img, conv_w.astype(img.dtype), window_strides=(1, 1), padding='SAME',
                                       dimension_numbers=('NHWC', 'HWIO', 'NHWC'), feature_group_count=3 * DN_W)
        q, k, v = jnp.split(jax.nn.silu(img).reshape(batch, length, 3 * DN_W), 3, axis=-1)
        q = l2norm(split_heads(q, H_DN)) * HD_DN ** -0.5
        k = l2norm(split_heads(k, H_DN))
        a = a.reshape(batch, length, 2, H_DN).astype(F32)
        b = b.reshape(batch, length, 2, H_DN).astype(F32)
        g = -jnp.exp(a_log.astype(F32)) * jax.nn.softplus(a + dt_bias.astype(F32))
        beta = jax.nn.sigmoid(b)
        return [q, k, split_heads(v, H_DN), g, beta], z

    seq_c, z_c = project(u_ctx, 1, n_ctx)
    seq_l, z_l = project(u_lat, rows, GRID_W)
    s0 = jnp.zeros((batch, H_DN, HD_DN, HD_DN), F32)
    o_ctx, o_lat = bidirectional(
        lambda d, s, st: gated_delta_scan(s[0], s[1], s[2], s[3][:, :, d], s[4][:, :, d], st), seq_c, seq_l, s0)

    def merge(o, z):
        length = o.shape[1]
        o = head_rmsnorm(o, norm_g) * jax.nn.silu(split_heads(z, H_DN).astype(F32))
        return o.reshape(batch, length, DN_W).astype(u_lat.dtype) @ w_out

    y_lat = merge(o_lat, z_l)
    y_ctx = merge(o_ctx, z_c) if need_ctx else None
    return y_lat, y_ctx


def swiglu(h, w_in, w_out):
    a, b = jnp.split(h @ w_in, 2, axis=-1)
    return (jax.nn.silu(a) * b) @ w_out


def moe_ffn(h, router_w, router_bias, exp_w_in, exp_w_out, sh_w_in, sh_w_out):
    n_tok = h.shape[0]
    scores = jax.nn.sigmoid((h @ router_w).astype(F32))
    biased = scores + router_bias.astype(F32)
    group_score = jnp.sum(lax.top_k(biased.reshape(n_tok, N_GROUPS, N_EXPERTS // N_GROUPS), 2)[0], -1)
    _, group_idx = lax.top_k(group_score, TOPK_GROUPS)
    group_mask = jnp.sum(jax.nn.one_hot(group_idx, N_GROUPS, dtype=F32), -2) > 0
    expert_mask = jnp.repeat(group_mask, N_EXPERTS // N_GROUPS, axis=-1)
    _, top_idx = lax.top_k(jnp.where(expert_mask, biased, -jnp.inf), TOP_K)
    top_w = jnp.take_along_axis(scores, top_idx, axis=-1)
    top_w = top_w / jnp.sum(top_w, -1, keepdims=True) * ROUTED_SCALE
    combine = jnp.einsum('tk,tke->te', top_w, jax.nn.one_hot(top_idx, N_EXPERTS, dtype=F32)).astype(h.dtype)
    out = swiglu(h, sh_w_in, sh_w_out)
    for e in range(N_EXPERTS):
        out = out + combine[:, e:e + 1] * swiglu(h, exp_w_in[e], exp_w_out[e])
    return out


def setup_inputs(seed: int = 0) -> dict:
    key = jax.random.key(seed)
    ks = jax.random.split(key, 32)
    d = D_MODEL

    def nrm(k, shape, scale):
        return jax.random.normal(k, shape, F32) * scale

    x = nrm(ks[0], (BATCH, SEQ, d), 1.0)
    c = nrm(ks[1], (BATCH, d), 1.0)
    ctx = nrm(ks[2], (BATCH, CTX_LEN, d), 1.0)
    c_ctx = nrm(ks[3], (d,), 1.0)
    ada_w = nrm(ks[4], (DEPTH, d, 6 * d), d ** -0.5)
    ada_b = nrm(ks[5], (DEPTH, 6 * d), 0.02)
    ln_g = 1.0 + nrm(ks[6], (DEPTH, 2, d), 0.02)
    ln_b = nrm(ks[7], (DEPTH, 2, d), 0.02)
    ab_w_in = nrm(ks[8], (N_EVEN, d, AB_IN_W), d ** -0.5)
    ab_w_out = nrm(ks[9], (N_EVEN, AB_MIX_W, d), DEEPNORM_BETA * AB_MIX_W ** -0.5)
    gamma = 1.0 - 2.0 ** (-5.0 - jnp.arange(H_RET, dtype=F32))
    ret_decay_rate = jnp.log(-jnp.log(gamma)) + nrm(ks[10], (N_EVEN, 2, H_RET), 0.1)
    ret_norm_g = 1.0 + nrm(ks[11], (N_EVEN, RET_W), 0.02)
    i_bias = nrm(ks[12], (N_EVEN, 2, 1, H_ML), 0.1)
    f_bias = jnp.linspace(3.0, 6.0, H_ML, dtype=F32) + nrm(ks[13], (N_EVEN, 2, 1, H_ML), 0.1)
    ml_gate_b = jnp.concatenate([i_bias, f_bias], axis=2)
    ml_norm_g = 1.0 + nrm(ks[14], (N_EVEN, ML_W), 0.02)
    dn_w_in = nrm(ks[15], (N_ODD, d, DN_IN_W), d ** -0.5)
    dn_conv_w = nrm(ks[16], (N_ODD, CONV_K, CONV_K, 1, 3 * DN_W), 1.0 / CONV_K)
    dn_a_log = jnp.log(jax.random.uniform(ks[17], (N_ODD, 2, H_DN), F32, 1.0, 16.0))
    dt = jnp.exp(jax.random.uniform(ks[18], (N_ODD, 2, H_DN), F32, math.log(1e-3), math.log(1e-1)))
    dn_dt_bias = dt + jnp.log(-jnp.expm1(-dt))
    dn_norm_g = 1.0 + nrm(ks[19], (N_ODD, HD_DN), 0.02)
    dn_w_out = nrm(ks[20], (N_ODD, DN_W, d), DEEPNORM_BETA * DN_W ** -0.5)
    router_w = nrm(ks[21], (DEPTH, d, N_EXPERTS), d ** -0.5)
    router_bias = nrm(ks[22], (DEPTH, N_EXPERTS), 0.01)
    exp_w_in = nrm(ks[23], (DEPTH, N_EXPERTS, d, 2 * EXPERT_HIDDEN), d ** -0.5)
    exp_w_out = nrm(ks[24], (DEPTH, N_EXPERTS, EXPERT_HIDDEN, d), DEEPNORM_BETA * EXPERT_HIDDEN ** -0.5)
    sh_w_in = nrm(ks[25], (DEPTH, d, 2 * SHARED_HIDDEN), d ** -0.5)
    sh_w_out = nrm(ks[26], (DEPTH, SHARED_HIDDEN, d), DEEPNORM_BETA * SHARED_HIDDEN ** -0.5)
    return {'x': x, 'c': c, 'ctx': ctx, 'c_ctx': c_ctx, 'ada_w': ada_w, 'ada_b': ada_b,
            'ln_g': ln_g, 'ln_b': ln_b, 'ab_w_in': ab_w_in, 'ab_w_out': ab_w_out,
            'ret_decay_rate': ret_decay_rate, 'ret_norm_g': ret_norm_g, 'ml_gate_b': ml_gate_b,
            'ml_norm_g': ml_norm_g, 'dn_w_in': dn_w_in, 'dn_conv_w': dn_conv_w, 'dn_a_log': dn_a_log,
            'dn_dt_bias': dn_dt_bias, 'dn_norm_g': dn_norm_g, 'dn_w_out': dn_w_out,
            'router_w': router_w, 'router_bias': router_bias, 'exp_w_in': exp_w_in,
            'exp_w_out': exp_w_out, 'sh_w_in': sh_w_in, 'sh_w_out': sh_w_out}


def reference(x, c, ctx, c_ctx, ada_w, ada_b, ln_g, ln_b, ab_w_in, ab_w_out, ret_decay_rate, ret_norm_g,
              ml_gate_b, ml_norm_g, dn_w_in, dn_conv_w, dn_a_log, dn_dt_bias, dn_norm_g, dn_w_out,
              router_w, router_bias, exp_w_in, exp_w_out, sh_w_in, sh_w_out):
    batch, n_lat, d = x.shape
    n_ctx = ctx.shape[1]
    silu_c = jax.nn.silu(c)
    silu_cc = jax.nn.silu(c_ctx)
    h_lat, h_ctx = x, ctx
    for layer in range(DEPTH):
        need_ctx = layer < DEPTH - 1
        sh1, sc1, g1, sh2, sc2, g2 = jnp.split((silu_c @ ada_w[layer] + ada_b[layer])[:, None, :], 6, axis=-1)
        csh1, csc1, cg1, csh2, csc2, cg2 = jnp.split(silu_cc @ ada_w[layer] + ada_b[layer], 6, axis=-1)

        u_lat = h_lat * (1.0 + sc1) + sh1
        u_ctx = h_ctx * (1.0 + csc1) + csh1
        j = layer // 2
        if layer % 2 == 0:
            y_lat, y_ctx = retention_mlstm_mixer(u_lat, u_ctx, ab_w_in[j], ab_w_out[j], ret_decay_rate[j],
                                                 ret_norm_g[j], ml_gate_b[j], ml_norm_g[j], need_ctx)
        else:
            y_lat, y_ctx = gated_deltanet_mixer(u_lat, u_ctx, dn_w_in[j], dn_conv_w[j], dn_a_log[j], dn_dt_bias[j],
                                                dn_norm_g[j], dn_w_out[j], need_ctx)
        h_lat = layer_norm(DEEPNORM_ALPHA * h_lat + g1 * y_lat, ln_g[layer, 0], ln_b[layer, 0])
        if need_ctx:
            h_ctx = layer_norm(DEEPNORM_ALPHA * h_ctx + cg1 * y_ctx, ln_g[layer, 0], ln_b[layer, 0])

        v_lat = (h_lat * (1.0 + sc2) + sh2).reshape(batch * n_lat, d)
        if need_ctx:
            v_ctx = (h_ctx * (1.0 + csc2) + csh2).reshape(batch * n_ctx, d)
            tokens = jnp.concatenate([v_lat, v_ctx], axis=0)
        else:
            tokens = v_lat
        f = moe_ffn(tokens, router_w[layer], router_bias[layer], exp_w_in[layer], exp_w_out[layer],
                    sh_w_in[layer], sh_w_out[layer])
        f_lat = f[:batch * n_lat].reshape(batch, n_lat, d)
        h_lat = layer_norm(DEEPNORM_ALPHA * h_lat + g2 * f_lat, ln_g[layer, 1], ln_b[layer, 1])
        if need_ctx:
            f_ctx = f[batch * n_lat:].reshape(batch, n_ctx, d)
            h_ctx = layer_norm(DEEPNORM_ALPHA * h_ctx + cg2 * f_ctx, ln_g[layer, 1], ln_b[layer, 1])
    return h_lat
```

```python
import functools
import math

import jax
import jax.numpy as jnp
from jax import lax
from jax.experimental import pallas as pl
from jax.experimental.pallas import tpu as pltpu

F32 = jnp.float32
BF16 = jnp.bfloat16
I32 = jnp.int32

V7X_VMEM_BYTES = 64 * 1024 * 1024
VMEM_LIMIT = V7X_VMEM_BYTES - 8 * 1024 * 1024
LANES = 128

GRID_W = 64
CHUNK = 64
TOP_K = 8
N_GROUPS = 8
TOPK_GROUPS = 4
ROUTED_SCALE = 2.5
ROPE_BASE = 10000.0
LN_EPS = 1e-5

LB = 256
TME = 256
MOD_ROWS = 8


def _cparams(sem):
    return pltpu.CompilerParams(dimension_semantics=sem, vmem_limit_bytes=VMEM_LIMIT)


def _bdot(a, b):
    return jnp.dot(a.astype(BF16), b.astype(BF16), preferred_element_type=F32)


def _bdot_nt(a, b):
    return lax.dot_general(a.astype(BF16), b.astype(BF16), (((1,), (1,)), ((), ())), preferred_element_type=F32)


def _bdot_tn(a, b):
    return lax.dot_general(a.astype(BF16), b.astype(BF16), (((0,), (0,)), ((), ())), preferred_element_type=F32)


def _fdot(a, b):
    return jnp.dot(a, b, preferred_element_type=F32, precision=lax.Precision.HIGHEST)


def _sigmoid(x):
    return 1.0 / (1.0 + jnp.exp(-x))


def _silu(x):
    return x * _sigmoid(x)


def _softplus(x):
    return jnp.maximum(x, 0.0) + jnp.log(1.0 + jnp.exp(-jnp.abs(x)))


def _log_sigmoid(x):
    return -_softplus(-x)


def _lane_col(x, lane):
    idx = lax.broadcasted_iota(I32, x.shape, 1)
    return jnp.sum(jnp.where(idx == lane, x, 0.0), axis=1, keepdims=True)


def _ada_kernel(c_ref, w_ref, b_ref, o_ref):
    o_ref[...] = _bdot(_silu(c_ref[...]), w_ref[...]) + b_ref[...]


def ada_table(cvec, ada_w, ada_b):
    depth, d, n = ada_w.shape
    tn = 512
    return pl.pallas_call(
        _ada_kernel,
        grid=(depth, n // tn),
        in_specs=[pl.BlockSpec((MOD_ROWS, d), lambda l, j: (0, 0)),
                  pl.BlockSpec((None, d, tn), lambda l, j: (l, 0, j)),
                  pl.BlockSpec((None, 1, tn), lambda l, j: (l, 0, j))],
        out_specs=pl.BlockSpec((None, MOD_ROWS, tn), lambda l, j: (l, 0, j)),
        out_shape=jax.ShapeDtypeStruct((depth, MOD_ROWS, n), F32),
        compiler_params=_cparams(("parallel", "parallel")),
        name="ada_table",
    )(cvec, ada_w, ada_b.reshape(depth, 1, n))


class Rows:
    def __init__(self, b, ctx, seq, d):
        self.b, self.ctx, self.seq, self.d = b, ctx, seq, d
        self.n_ctx = b * ctx
        self.t = b * ctx + b * seq
        self.tm = math.gcd(math.gcd(self.n_ctx, seq), 512)

    def seg(self, i, tm):
        n_ctx_tiles = self.n_ctx // tm
        return jnp.where(i < n_ctx_tiles, self.b, (i - n_ctx_tiles) // (self.seq // tm))

    def mod_spec(self, layer, which, tm):
        return pl.BlockSpec((None, 1, self.d), lambda i: ((layer * MOD_ROWS + self.seg(i, tm)) * 6 + which, 0, 0))

    def row_spec(self, tm, width=None):
        return pl.BlockSpec((tm, self.d if width is None else width), lambda i: (i, 0))


def _mod_kernel(h_ref, sc_ref, sh_ref, o_ref):
    o_ref[...] = (h_ref[...] * (1.0 + sc_ref[...]) + sh_ref[...]).astype(o_ref.dtype)


def modulate(rows, h, modv, layer, sc_idx, sh_idx):
    tm = rows.tm
    return pl.pallas_call(
        _mod_kernel,
        grid=(rows.t // tm,),
        in_specs=[rows.row_spec(tm), rows.mod_spec(layer, sc_idx, tm), rows.mod_spec(layer, sh_idx, tm)],
        out_specs=rows.row_spec(tm),
        out_shape=jax.ShapeDtypeStruct((rows.t, rows.d), BF16),
        compiler_params=_cparams(("parallel",)),
        name="modulate",
    )(h, modv, modv)


def _layer_norm_rows(r, g, b):
    mu = jnp.mean(r, axis=-1, keepdims=True)
    xc = r - mu
    var = jnp.mean(xc * xc, axis=-1, keepdims=True)
    return xc * lax.rsqrt(var + LN_EPS) * g + b


def _ln_kernel(alpha, h_ref, z_ref, gate_ref, lng_ref, lnb_ref, sc_ref, sh_ref, hn_ref, v_ref):
    y = _layer_norm_rows(alpha * h_ref[...] + gate_ref[...] * z_ref[...], lng_ref[...], lnb_ref[...])
    hn_ref[...] = y
    v_ref[...] = (y * (1.0 + sc_ref[...]) + sh_ref[...]).astype(v_ref.dtype)


def ln_residual(rows, h, z, modv, layer, gate_idx, lng, lnb, alpha, sc_idx, sh_idx, mod_layer, v_dtype):
    tm = rows.tm // 2
    d = rows.d
    vec = pl.BlockSpec((1, d), lambda i: (0, 0))
    return pl.pallas_call(
        functools.partial(_ln_kernel, alpha),
        grid=(rows.t // tm,),
        in_specs=[rows.row_spec(tm), rows.row_spec(tm), rows.mod_spec(layer, gate_idx, tm), vec, vec,
                  rows.mod_spec(mod_layer, sc_idx, tm), rows.mod_spec(mod_layer, sh_idx, tm)],
        out_specs=[rows.row_spec(tm), rows.row_spec(tm)],
        out_shape=[jax.ShapeDtypeStruct((rows.t, d), F32), jax.ShapeDtypeStruct((rows.t, d), v_dtype)],
        compiler_params=_cparams(("parallel",)),
        name="ln_residual",
    )(h, z, modv, lng.reshape(1, d), lnb.reshape(1, d), modv, modv)


def _mm_kernel(x_ref, w_ref, o_ref):
    o_ref[...] = _bdot(x_ref[...], w_ref[...]).astype(o_ref.dtype)


def _pick_tile(n, prefs):
    for p in prefs:
        if n % p == 0:
            return p
    return n


def matmul(x, w, out_dtype=F32, name="matmul"):
    m, k = x.shape
    n = w.shape[1]
    tm = _pick_tile(m, (768, 512, 256, 128))
    tn = _pick_tile(n, (1024, 512, 256, 128))
    return pl.pallas_call(
        _mm_kernel,
        grid=(n // tn, m // tm),
        in_specs=[pl.BlockSpec((tm, k), lambda j, i: (i, 0)), pl.BlockSpec((k, tn), lambda j, i: (0, j))],
        out_specs=pl.BlockSpec((tm, tn), lambda j, i: (i, j)),
        out_shape=jax.ShapeDtypeStruct((m, n), out_dtype),
        compiler_params=_cparams(("parallel", "parallel")),
        name=name,
    )(x, w)


def _mm2_kernel(x1_ref, x2_ref, w1_ref, w2_ref, o_ref):
    o_ref[...] = _bdot(x1_ref[...], w1_ref[...]) + _bdot(x2_ref[...], w2_ref[...])


def matmul2(x1, x2, w, name="matmul2"):
    m, k1 = x1.shape
    k2 = x2.shape[1]
    n = w.shape[1]
    tm = _pick_tile(m, (768, 512, 256, 128))
    tn = _pick_tile(n, (1024, 512, 256, 128))
    return pl.pallas_call(
        _mm2_kernel,
        grid=(n // tn, m // tm),
        in_specs=[pl.BlockSpec((tm, k1), lambda j, i: (i, 0)), pl.BlockSpec((tm, k2), lambda j, i: (i, 0)),
                  pl.BlockSpec((k1, tn), lambda j, i: (0, j)), pl.BlockSpec((k2, tn), lambda j, i: (k1 // k2, j))],
        out_specs=pl.BlockSpec((tm, tn), lambda j, i: (i, j)),
        out_shape=jax.ShapeDtypeStruct((m, n), F32),
        compiler_params=_cparams(("parallel", "parallel")),
        name=name,
    )(x1, x2, w, w)


class ScanOrder:
    def __init__(self, b, ctx, seq):
        assert ctx % LB == 0 and seq % LB == 0
        self.b = b
        self.nc = ctx // LB
        self.nl = seq // LB
        self.steps = self.nc + self.nl

    def row_block(self, direction, bi, s):
        nc, nl = self.nc, self.nl
        if direction == 0:
            c, l = s, s - nc
        else:
            c, l = nc - 1 - s, nl - 1 - (s - nc)
        return jnp.where(s < nc, bi * nc + c, self.b * nc + bi * nl + l)

    def pos_block(self, direction, s):
        nc, nl = self.nc, self.nl
        if direction == 0:
            return s
        return jnp.where(s < nc, nc - 1 - s, nc + nl - 1 - (s - nc))


def _causal(direction, n):
    ri = lax.broadcasted_iota(I32, (n, n), 0)
    ci = lax.broadcasted_iota(I32, (n, n), 1)
    rel = ri - ci if direction == 0 else ci - ri
    return rel >= 0, rel > 0, rel


def _rope(t, cos, sin):
    half = t.shape[-1] // 2
    t1, t2 = t[:, :half], t[:, half:]
    return jnp.concatenate([t1 * cos - t2 * sin, t1 * sin + t2 * cos], axis=-1)


def _head_ln_gate(o, g, gate_act):
    mu = jnp.mean(o, axis=-1, keepdims=True)
    xc = o - mu
    var = jnp.mean(xc * xc, axis=-1, keepdims=True)
    return xc * lax.rsqrt(var + LN_EPS) * g * gate_act


def _ret_kernel(direction, hd, *refs):
    if direction == 0:
        q_ref, k_ref, v_ref, cos_ref, sin_ref, rate_ref, o_ref, s_ref = refs
    else:
        q_ref, k_ref, v_ref, cos_ref, sin_ref, rate_ref, of_ref, g_ref, ng_ref, o_ref, s_ref = refs
    step = pl.program_id(2)

    @pl.when(step == 0)
    def _():
        s_ref[...] = jnp.zeros_like(s_ref)

    lg = -jnp.exp(rate_ref[...])[:, :1]
    incl, _, rel = _causal(direction, LB)
    ri = lax.broadcasted_iota(I32, (LB, 1), 0).astype(F32)
    if direction == 0:
        q_dec = jnp.exp(lg * (ri + 1.0))
        k_dec = jnp.exp(lg * (LB - 1.0 - ri))
    else:
        q_dec = jnp.exp(lg * (LB - ri))
        k_dec = jnp.exp(lg * ri)
    intra = jnp.where(incl, jnp.exp(lg * jnp.maximum(rel, 0).astype(F32)), 0.0)
    c_dec = jnp.exp(lg * LB)

    cos, sin = cos_ref[...], sin_ref[...]
    q = _rope(q_ref[...], cos, sin)
    k = _rope(k_ref[...], cos, sin) * hd ** -0.5
    v = v_ref[...]
    s = s_ref[...]
    att = _bdot_nt(q, k) * intra
    o = _bdot(att, v) + _bdot(q * q_dec, s)
    s_ref[...] = s * c_dec + _bdot_tn(k * k_dec, v)
    if direction == 0:
        o_ref[...] = o
    else:
        o_ref[...] = _head_ln_gate(o + of_ref[...], ng_ref[...], _silu(g_ref[...])).astype(o_ref.dtype)


def retention(order, p, cos, sin, rate, norm_g, nh, hd):
    t = p.shape[0]
    b = order.b
    half = hd // 2

    def run(direction, extra_in, extra_specs, out_dtype):
        def blk(col0):
            return pl.BlockSpec((LB, hd), lambda bi, h, s: (order.row_block(direction, bi, s), col0 + h))
        in_specs = [blk(0), blk(nh), blk(2 * nh),
                    pl.BlockSpec((LB, half), lambda bi, h, s: (order.pos_block(direction, s), 0)),
                    pl.BlockSpec((LB, half), lambda bi, h, s: (order.pos_block(direction, s), 0)),
                    pl.BlockSpec((None, 1, LANES), lambda bi, h, s: (direction * nh + h, 0, 0))] + extra_specs
        return pl.pallas_call(
            functools.partial(_ret_kernel, direction, hd),
            grid=(b, nh, order.steps),
            in_specs=in_specs,
            out_specs=pl.BlockSpec((LB, hd), lambda bi, h, s: (order.row_block(direction, bi, s), h)),
            out_shape=jax.ShapeDtypeStruct((t, nh * hd), out_dtype),
            scratch_shapes=[pltpu.VMEM((hd, hd), F32)],
            compiler_params=_cparams(("parallel", "parallel", "arbitrary")),
            name=f"retention_d{direction}",
        )(p, p, p, cos, sin, rate, *extra_in)

    o_fwd = run(0, [], [], F32)
    return run(1, [o_fwd, p, norm_g.reshape(1, nh * hd)],
               [pl.BlockSpec((LB, hd), lambda bi, h, s: (order.row_block(1, bi, s), h)),
                pl.BlockSpec((LB, hd), lambda bi, h, s: (order.row_block(1, bi, s), 3 * nh + h)),
                pl.BlockSpec((1, hd), lambda bi, h, s: (0, h))], BF16)


def _ml_gate_kernel(nh, g_ref, b_ref, gc_ref, gr_ref):
    g = g_ref[...] + b_ref[...]
    ls = _log_sigmoid(g)
    incl_f, _, _ = _causal(0, LB)
    incl_b, _, _ = _causal(1, LB)
    pre = _fdot(incl_f.astype(F32), ls)
    suf = _fdot(incl_b.astype(F32), ls)
    lane = lax.broadcasted_iota(I32, g.shape, 1)
    is_f = (lane // nh) % 2 == 1
    is_d1 = lane // (2 * nh) == 1
    out = jnp.where(is_f, jnp.where(is_d1, suf, pre), g)
    gc_ref[...] = out
    gr_ref[...] = out.T


def ml_gates(pg, bias_row, nh):
    t = pg.shape[0]
    return pl.pallas_call(
        functools.partial(_ml_gate_kernel, nh),
        grid=(t // LB,),
        in_specs=[pl.BlockSpec((LB, LANES), lambda i: (i, 0)), pl.BlockSpec((1, LANES), lambda i: (0, 0))],
        out_specs=[pl.BlockSpec((LB, LANES), lambda i: (i, 0)), pl.BlockSpec((None, LANES, LB), lambda i: (i, 0, 0))],
        out_shape=[jax.ShapeDtypeStruct((t, LANES), F32), jax.ShapeDtypeStruct((t // LB, LANES, LB), F32)],
        compiler_params=_cparams(("parallel",)),
        name="ml_gates",
    )(pg, bias_row)


def _ml_kernel(direction, nh, hd, *refs):
    if direction == 0:
        q_ref, k_ref, v_ref, gc_ref, gr_ref, o_ref, c_ref, n_ref, m_ref = refs
    else:
        q_ref, k_ref, v_ref, gc_ref, gr_ref, of_ref, g_ref, ng_ref, o_ref, c_ref, n_ref, m_ref = refs
    h = pl.program_id(1)
    step = pl.program_id(2)

    @pl.when(step == 0)
    def _():
        c_ref[...] = jnp.zeros_like(c_ref)
        n_ref[...] = jnp.zeros_like(n_ref)
        m_ref[...] = jnp.zeros_like(m_ref)

    lane_i = direction * 2 * nh + h
    lane_f = lane_i + nh
    gc = gc_ref[...]
    ig_col = _lane_col(gc, lane_i)
    b_col = _lane_col(gc, lane_f)
    ig_row = gr_ref[pl.ds(lane_i, 1), :]
    b_row = gr_ref[pl.ds(lane_f, 1), :]
    m_prev = m_ref[0:1, 0:1]

    incl, _, _ = _causal(direction, LB)
    d_log = jnp.where(incl, b_col - b_row + ig_row, -jnp.inf)
    inter_log = b_col + m_prev
    m_row = jnp.maximum(inter_log, jnp.max(d_log, axis=1, keepdims=True))
    d_w = jnp.exp(d_log - m_row)
    inter_w = jnp.exp(inter_log - m_row)

    q = q_ref[...]
    k = k_ref[...] * hd ** -0.5
    v = v_ref[...]
    c_mat = c_ref[...]
    n_row = n_ref[0:1, :]
    s = _bdot_nt(q, k) * d_w
    num = _bdot(s, v) + inter_w * _bdot(q, c_mat)
    den = jnp.sum(s, axis=1, keepdims=True) + inter_w * jnp.sum(q * n_row, axis=1, keepdims=True)
    hout = num / jnp.maximum(jnp.abs(den), jnp.exp(-m_row))

    last = LB - 1 if direction == 0 else 0
    b_last = b_col[last:last + 1, :]
    m_last = m_row[last:last + 1, :]
    dec_last = inter_w[last:last + 1, :]
    w_last = jnp.exp(b_last - b_col + ig_col - m_last)
    kw = k * w_last
    c_ref[...] = dec_last * c_mat + _bdot_tn(kw, v)
    n_ref[0:1, :] = dec_last * n_row + jnp.sum(kw, axis=0, keepdims=True)
    m_ref[...] = jnp.broadcast_to(m_last, m_ref.shape)
    if direction == 0:
        o_ref[...] = hout
    else:
        o_ref[...] = _head_ln_gate(hout + of_ref[...], ng_ref[...], _sigmoid(g_ref[...])).astype(o_ref.dtype)


def mlstm(order, p, col0, gc, gr, norm_g, nh, hd):
    t = p.shape[0]
    b = order.b

    def run(direction, extra_in, extra_specs, out_dtype):
        def blk(c):
            return pl.BlockSpec((LB, hd), lambda bi, h, s: (order.row_block(direction, bi, s), col0 + c + h))
        in_specs = [blk(0), blk(nh), blk(2 * nh),
                    pl.BlockSpec((LB, LANES), lambda bi, h, s: (order.row_block(direction, bi, s), 0)),
                    pl.BlockSpec((None, LANES, LB), lambda bi, h, s: (order.row_block(direction, bi, s), 0, 0))]
        return pl.pallas_call(
            functools.partial(_ml_kernel, direction, nh, hd),
            grid=(b, nh, order.steps),
            in_specs=in_specs + extra_specs,
            out_specs=pl.BlockSpec((LB, hd), lambda bi, h, s: (order.row_block(direction, bi, s), h)),
            out_shape=jax.ShapeDtypeStruct((t, nh * hd), out_dtype),
            scratch_shapes=[pltpu.VMEM((hd, hd), F32), pltpu.VMEM((8, hd), F32), pltpu.VMEM((8, LANES), F32)],
            compiler_params=_cparams(("parallel", "parallel", "arbitrary")),
            name=f"mlstm_d{direction}",
        )(p, p, p, gc, gr, *extra_in)

    o_fwd = run(0, [], [], F32)
    return run(1, [o_fwd, p, norm_g.reshape(1, nh * hd)],
               [pl.BlockSpec((LB, hd), lambda bi, h, s: (order.row_block(1, bi, s), h)),
                pl.BlockSpec((LB, hd), lambda bi, h, s: (order.row_block(1, bi, s), col0 + 3 * nh + h)),
                pl.BlockSpec((1, hd), lambda bi, h, s: (0, h))], BF16)


CONV_HALO = 128


def _conv_kernel(tb, tc, n_ctx, ctx, seq, n_qk_tiles, n_q_tiles, hd, xp_ref, xc_ref, xn_ref, w_ref, o_ref, buf_ref):
    i = pl.program_id(0)
    j = pl.program_id(1)
    buf_ref[0:CONV_HALO, :] = xp_ref[...]
    buf_ref[CONV_HALO:CONV_HALO + tb, :] = xc_ref[...]
    buf_ref[CONV_HALO + tb:, :] = xn_ref[...]

    row = i * tb + lax.broadcasted_iota(I32, (tb, 1), 0)
    is_ctx = i * tb < n_ctx
    li = jnp.where(is_ctx, row % ctx, (row - n_ctx) % seq)
    rr = jnp.where(is_ctx, 0, li // GRID_W)
    cc = jnp.where(is_ctx, li, li % GRID_W)
    n_r = jnp.where(is_ctx, 1, seq // GRID_W)
    n_c = jnp.where(is_ctx, ctx, GRID_W)

    acc = jnp.zeros((tb, tc), F32)
    for dr in range(3):
        for dc in range(3):
            off = (dr - 1) * GRID_W + (dc - 1)
            r2 = rr + (dr - 1)
            c2 = cc + (dc - 1)
            ok = (r2 >= 0) & (r2 < n_r) & (c2 >= 0) & (c2 < n_c)
            src = buf_ref[CONV_HALO + off:CONV_HALO + off + tb, :]
            acc = acc + jnp.where(ok, src, 0.0) * w_ref[dr * 3 + dc:dr * 3 + dc + 1, :]
    y = _silu(acc)
    qscale = jnp.where(j < n_q_tiles, hd ** -0.5, 1.0)
    is_qk = j < n_qk_tiles
    for g in range(tc // hd):
        yg = y[:, g * hd:(g + 1) * hd]
        nrm = lax.rsqrt(jnp.sum(yg * yg, axis=1, keepdims=True) + 1e-6) * qscale
        o_ref[:, g * hd:(g + 1) * hd] = yg * jnp.where(is_qk, nrm, 1.0)


def dn_conv(rows, p, conv_w9, dn_w, hd):
    t = rows.t
    tb = rows.tm
    tc = 512
    hb = tb // CONV_HALO
    n_halo = t // CONV_HALO
    kern = functools.partial(_conv_kernel, tb, tc, rows.n_ctx, rows.ctx, rows.seq, 2 * dn_w // tc, dn_w // tc, hd)
    return pl.pallas_call(
        kern,
        grid=(t // tb, 3 * dn_w // tc),
        in_specs=[pl.BlockSpec((CONV_HALO, tc), lambda i, j: (jnp.maximum(i * hb - 1, 0), j)),
                  pl.BlockSpec((tb, tc), lambda i, j: (i, j)),
                  pl.BlockSpec((CONV_HALO, tc), lambda i, j: (jnp.minimum(i * hb + hb, n_halo - 1), j)),
                  pl.BlockSpec((16, tc), lambda i, j: (0, j))],
        out_specs=pl.BlockSpec((tb, tc), lambda i, j: (i, j)),
        out_shape=jax.ShapeDtypeStruct((t, 3 * dn_w), F32),
        scratch_shapes=[pltpu.VMEM((tb + 2 * CONV_HALO, tc), F32)],
        compiler_params=_cparams(("parallel", "parallel")),
        name="dn_conv",
    )(p, p, p, conv_w9)


def _chunk_masks(direction):
    ri = lax.broadcasted_iota(I32, (LB, LB), 0)
    ci = lax.broadcasted_iota(I32, (LB, LB), 1)
    same = (ri // CHUNK) == (ci // CHUNK)
    rel = ri - ci if direction == 0 else ci - ri
    return same & (rel >= 0), same & (rel > 0), same


def _dn_gate_kernel(nh, ab_ref, alog_ref, dtb_ref, gc_ref, gr_ref, gt_ref):
    x = ab_ref[...]
    lane = lax.broadcasted_iota(I32, x.shape, 1)
    g = -jnp.exp(alog_ref[...]) * _softplus(x + dtb_ref[...])
    g = jnp.where(lane < 2 * nh, g, 0.0)
    incl_f, _, same = _chunk_masks(0)
    incl_b, _, _ = _chunk_masks(1)
    pre = _fdot(incl_f.astype(F32), g)
    suf = _fdot(incl_b.astype(F32), g)
    tot = _fdot(same.astype(F32), g)
    out = jnp.where(lane < nh, pre, jnp.where(lane < 2 * nh, suf, _sigmoid(x)))
    gc_ref[...] = out
    gr_ref[...] = out.T
    gt_ref[...] = tot


def dn_gates(pg, alog_row, dtb_row, nh):
    t = pg.shape[0]
    return pl.pallas_call(
        functools.partial(_dn_gate_kernel, nh),
        grid=(t // LB,),
        in_specs=[pl.BlockSpec((LB, LANES), lambda i: (i, 0)), pl.BlockSpec((1, LANES), lambda i: (0, 0)),
                  pl.BlockSpec((1, LANES), lambda i: (0, 0))],
        out_specs=[pl.BlockSpec((LB, LANES), lambda i: (i, 0)), pl.BlockSpec((None, LANES, LB), lambda i: (i, 0, 0)),
                   pl.BlockSpec((LB, LANES), lambda i: (i, 0))],
        out_shape=[jax.ShapeDtypeStruct((t, LANES), F32), jax.ShapeDtypeStruct((t // LB, LANES, LB), F32),
                   jax.ShapeDtypeStruct((t, LANES), F32)],
        compiler_params=_cparams(("parallel",)),
        name="dn_gates",
    )(pg, alog_row, dtb_row)


DN_HG = 4


def _dn_ut_kernel(direction, nh, hd, q_ref, k_ref, v_ref, gc_ref, gr_ref, gt_ref, u_ref, w_ref, qg_ref, kd_ref, att_ref):
    hg = pl.program_id(1)
    incl, strict, _ = _chunk_masks(direction)
    ri = lax.broadcasted_iota(I32, (LB, LB), 0)
    ci = lax.broadcasted_iota(I32, (LB, LB), 1)
    eye = (ri == ci).astype(F32)
    gcb = gc_ref[...]
    gtb = gt_ref[...]
    for hh in range(DN_HG):
        lane_g = direction * nh + hg * DN_HG + hh
        gc_col = _lane_col(gcb, lane_g)
        beta = _lane_col(gcb, 2 * nh + lane_g)
        gt_col = _lane_col(gtb, lane_g)
        gc_row = gr_ref[pl.ds(lane_g, 1), :]
        sl = slice(hh * hd, (hh + 1) * hd)
        q, k, v = q_ref[:, sl], k_ref[:, sl], v_ref[:, sl]
        decay = jnp.exp(jnp.where(incl, gc_col - gc_row, -jnp.inf))
        kb = k * beta
        kk = _bdot_nt(kb, k)
        a = jnp.where(strict, kk * decay, 0.0)
        tinv = eye - jnp.where((ri // 2) == (ci // 2), a, 0.0)
        s = 2
        while s < CHUNK:
            a_s = jnp.where(((ri // (2 * s)) == (ci // (2 * s))) & ((ri // s) != (ci // s)), a, 0.0)
            tinv = tinv - _bdot(_bdot(tinv, a_s), tinv)
            s *= 2
        eg = jnp.exp(gc_col)
        u_ref[:, sl] = _bdot(tinv, v * beta)
        w_ref[:, sl] = _bdot(tinv, kb * eg).astype(w_ref.dtype)
        qg_ref[:, sl] = (q * eg).astype(qg_ref.dtype)
        kd_ref[:, sl] = (k * jnp.exp(gt_col - gc_col)).astype(kd_ref.dtype)
        att_ref[:, hh * LB:(hh + 1) * LB] = (_bdot_nt(q, k) * decay).astype(att_ref.dtype)


def _dn_scan_kernel(direction, nh, hd, *refs):
    if direction == 0:
        u_ref, w_ref, qg_ref, kd_ref, att_ref, gt_ref, o_ref, s_ref, vn_ref = refs
    else:
        u_ref, w_ref, qg_ref, kd_ref, att_ref, gt_ref, of_ref, z_ref, ng_ref, o_ref, s_ref, vn_ref = refs
    hg = pl.program_id(1)
    step = pl.program_id(2)

    @pl.when(step == 0)
    def _():
        s_ref[...] = jnp.zeros_like(s_ref)

    vn_ref[...] = jnp.zeros_like(vn_ref)
    gtb = gt_ref[...]
    n_chunks = LB // CHUNK
    for hh in range(DN_HG):
        lane_g = direction * nh + hg * DN_HG + hh
        eg_col = jnp.exp(_lane_col(gtb, lane_g))
        sl = slice(hh * hd, (hh + 1) * hd)
        s = s_ref[hh]
        for ci in range(n_chunks):
            cc = ci if direction == 0 else n_chunks - 1 - ci
            rs = slice(cc * CHUNK, (cc + 1) * CHUNK)
            v_new = u_ref[rs, sl] - _bdot(w_ref[rs, sl], s)
            vn_ref[hh, rs, :] = v_new
            o = _bdot(qg_ref[rs, sl], s) + _bdot(att_ref[rs, hh * LB:(hh + 1) * LB], vn_ref[hh])
            s = eg_col[cc * CHUNK:cc * CHUNK + 1, :] * s + _bdot_tn(kd_ref[rs, sl], v_new)
            if direction == 0:
                o_ref[rs, sl] = o
            else:
                o = o + of_ref[rs, sl]
                rms = lax.rsqrt(jnp.mean(o * o, axis=1, keepdims=True) + 1e-6)
                o_ref[rs, sl] = (o * rms * ng_ref[...] * _silu(z_ref[rs, sl])).astype(o_ref.dtype)
        s_ref[hh] = s


def gated_deltanet(order, qkv, p, zcol0, gc, gr, gt, norm_g, nh, hd):
    t = qkv.shape[0]
    b = order.b
    w = nh * hd
    gw = DN_HG * hd
    ng = nh // DN_HG

    outs = []
    for direction in (0, 1):
        ut = pl.pallas_call(
            functools.partial(_dn_ut_kernel, direction, nh, hd),
            grid=(t // LB, ng),
            in_specs=[pl.BlockSpec((LB, gw), lambda i, g: (i, g)), pl.BlockSpec((LB, gw), lambda i, g: (i, ng + g)),
                      pl.BlockSpec((LB, gw), lambda i, g: (i, 2 * ng + g)),
                      pl.BlockSpec((LB, LANES), lambda i, g: (i, 0)),
                      pl.BlockSpec((None, LANES, LB), lambda i, g: (i, 0, 0)),
                      pl.BlockSpec((LB, LANES), lambda i, g: (i, 0))],
            out_specs=[pl.BlockSpec((LB, gw), lambda i, g: (i, g))] * 4 + [pl.BlockSpec((LB, DN_HG * LB), lambda i, g: (i, g))],
            out_shape=[jax.ShapeDtypeStruct((t, w), F32)] + [jax.ShapeDtypeStruct((t, w), BF16)] * 3
                      + [jax.ShapeDtypeStruct((t, nh * LB), BF16)],
            compiler_params=_cparams(("parallel", "parallel")),
            name=f"dn_ut_d{direction}",
        )(qkv, qkv, qkv, gc, gr, gt)
        outs.append(ut)

    def run(direction, extra_in, extra_specs, out_dtype):
        def blk(width):
            return pl.BlockSpec((LB, width), lambda bi, g, s: (order.row_block(direction, bi, s), g))
        return pl.pallas_call(
            functools.partial(_dn_scan_kernel, direction, nh, hd),
            grid=(b, ng, order.steps),
            in_specs=[blk(gw)] * 4 + [blk(DN_HG * LB),
                      pl.BlockSpec((LB, LANES), lambda bi, g, s: (order.row_block(direction, bi, s), 0))] + extra_specs,
            out_specs=blk(gw),
            out_shape=jax.ShapeDtypeStruct((t, w), out_dtype),
            scratch_shapes=[pltpu.VMEM((DN_HG, hd, hd), F32), pltpu.VMEM((DN_HG, LB, hd), F32)],
            compiler_params=_cparams(("parallel", "parallel", "arbitrary")),
            name=f"dn_scan_d{direction}",
        )(*outs[direction], gt, *extra_in)

    o_fwd = run(0, [], [], F32)
    return run(1, [o_fwd, p, norm_g.reshape(1, hd)],
               [pl.BlockSpec((LB, gw), lambda bi, g, s: (order.row_block(1, bi, s), g)),
                pl.BlockSpec((LB, gw), lambda bi, g, s: (order.row_block(1, bi, s), zcol0 + g)),
                pl.BlockSpec((1, hd), lambda bi, g, s: (0, 0))], BF16)


def _router_kernel(tm, ne, x_ref, w_ref, bias_ref, eid_ref, pos_ref, wgt_ref, cnt_ref, run_ref):
    i = pl.program_id(0)

    @pl.when(i == 0)
    def _():
        run_ref[...] = jnp.zeros_like(run_ref)

    scores = _sigmoid(_fdot(x_ref[...], w_ref[...]))
    biased = scores + bias_ref[...]
    lane = lax.broadcasted_iota(I32, (tm, ne), 1)
    gsz = ne // N_GROUPS
    grp = lane // gsz
    neg = -jnp.inf

    gs = []
    for g in range(N_GROUPS):
        m = jnp.where(grp == g, biased, neg)
        top1 = jnp.max(m, axis=1, keepdims=True)
        n_top = jnp.sum((m == top1).astype(F32), axis=1, keepdims=True)
        rest = jnp.max(jnp.where(m < top1, m, neg), axis=1, keepdims=True)
        gs.append(top1 + jnp.where(n_top >= 2.0, top1, rest))
    emask = jnp.zeros((tm, ne), jnp.bool_)
    for g in range(N_GROUPS):
        beaten = jnp.zeros((tm, 1), F32)
        for g2 in range(N_GROUPS):
            if g2 == g:
                continue
            wins = (gs[g2] > gs[g]) | (gs[g2] == gs[g]) if g2 < g else gs[g2] > gs[g]
            beaten = beaten + wins.astype(F32)
        emask = emask | ((grp == g) & (beaten < TOPK_GROUPS))
    masked = jnp.where(emask, biased, neg)

    sel = jnp.zeros((tm, ne), jnp.bool_)
    idxs = []
    lane_f = lane.astype(F32)
    for _ in range(TOP_K):
        mx = jnp.max(masked, axis=1, keepdims=True)
        idx = jnp.min(jnp.where(masked == mx, lane_f, float(ne)), axis=1, keepdims=True).astype(I32)
        hit = lane == idx
        sel = sel | hit
        masked = jnp.where(hit, neg, masked)
        idxs.append(idx)
    w_sel = jnp.where(sel, scores, 0.0)
    w_sel = w_sel / jnp.sum(w_sel, axis=1, keepdims=True) * ROUTED_SCALE

    ri = lax.broadcasted_iota(I32, (tm, tm), 0)
    ci = lax.broadcasted_iota(I32, (tm, tm), 1)
    self = sel.astype(F32)
    before = _bdot((ri > ci).astype(F32), self) + run_ref[0:1, :]
    run_ref[0:1, :] = run_ref[0:1, :] + jnp.sum(self, axis=0, keepdims=True)
    cnt_ref[...] = jnp.broadcast_to(run_ref[0:1, :], cnt_ref.shape)

    lane_o = lax.broadcasted_iota(I32, (tm, LANES), 1)
    eid = jnp.zeros((tm, LANES), I32)
    pos = jnp.zeros((tm, LANES), I32)
    wgt = jnp.zeros((tm, LANES), F32)
    for kk, idx in enumerate(idxs):
        hit = lane == idx
        p_k = jnp.sum(jnp.where(hit, before, 0.0), axis=1, keepdims=True).astype(I32)
        w_k = jnp.sum(jnp.where(hit, w_sel, 0.0), axis=1, keepdims=True)
        eid = jnp.where(lane_o == kk, idx, eid)
        pos = jnp.where(lane_o == kk, p_k, pos)
        wgt = jnp.where(lane_o == kk, w_k, wgt)
    eid_ref[...] = eid
    pos_ref[...] = pos
    wgt_ref[...] = wgt


def router(v32, router_w, router_bias):
    t, d = v32.shape
    ne = router_w.shape[1]
    tm = _pick_tile(t, (256, 128))
    row = pl.BlockSpec((tm, LANES), lambda i: (i, 0))
    return pl.pallas_call(
        functools.partial(_router_kernel, tm, ne),
        grid=(t // tm,),
        in_specs=[pl.BlockSpec((tm, d), lambda i: (i, 0)), pl.BlockSpec((d, ne), lambda i: (0, 0)),
                  pl.BlockSpec((1, ne), lambda i: (0, 0))],
        out_specs=[row, row, row, pl.BlockSpec((8, ne), lambda i: (0, 0))],
        out_shape=[jax.ShapeDtypeStruct((t, LANES), I32), jax.ShapeDtypeStruct((t, LANES), I32),
                   jax.ShapeDtypeStruct((t, LANES), F32), jax.ShapeDtypeStruct((8, ne), F32)],
        scratch_shapes=[pltpu.VMEM((8, ne), F32)],
        compiler_params=_cparams(("arbitrary",)),
        name="router",
    )(v32, router_w, router_bias.reshape(1, ne))


def _dispatch_kernel(tm, dest_ref, x_ref, xs_in_ref, xs_ref, sem):
    del xs_in_ref

    def row_copy(t, kk):
        return pltpu.make_async_copy(x_ref.at[pl.ds(t, 1), :], xs_ref.at[pl.ds(dest_ref[0, t * TOP_K + kk], 1), :], sem)

    def issue(t, c):
        for kk in range(TOP_K):
            row_copy(t, kk).start()
        return c

    lax.fori_loop(0, tm, issue, 0)

    def drain(t, c):
        for kk in range(TOP_K):
            row_copy(t, kk).wait()
        return c

    lax.fori_loop(0, tm, drain, 0)


def dispatch(v32, dest, n_rows):
    t, d = v32.shape
    tm = _pick_tile(t, (256, 128))
    xs0 = jnp.zeros((n_rows, d), F32)
    return pl.pallas_call(
        functools.partial(_dispatch_kernel, tm),
        grid=(t // tm,),
        in_specs=[pl.BlockSpec((None, 1, tm * TOP_K), lambda i: (i, 0, 0), memory_space=pltpu.SMEM),
                  pl.BlockSpec((tm, d), lambda i: (i, 0)),
                  pl.BlockSpec(memory_space=pl.ANY)],
        out_specs=pl.BlockSpec(memory_space=pl.ANY),
        out_shape=jax.ShapeDtypeStruct((n_rows, d), F32),
        scratch_shapes=[pltpu.SemaphoreType.DMA],
        input_output_aliases={2: 0},
        compiler_params=_cparams(("arbitrary",)),
        name="moe_dispatch",
    )(dest.reshape(t // tm, 1, tm * TOP_K), v32, xs0)


def _swiglu_kernel(eh, te_ref, nt_ref, x_ref, wi_ref, wo_ref, o_ref):
    del te_ref

    @pl.when(pl.program_id(0) < nt_ref[0])
    def _():
        ab = _bdot(x_ref[...], wi_ref[...])
        act = _silu(ab[:, :eh]) * ab[:, eh:]
        o_ref[...] = _bdot(act, wo_ref[...]).astype(o_ref.dtype)


def grouped_swiglu(xs, w_in, w_out, tile_expert, n_tiles, out_dtype=F32):
    r, d = xs.shape
    ne, _, eh2 = w_in.shape
    eh = eh2 // 2
    grid_n = r // TME

    def row_map(j, te, nt):
        return (jnp.minimum(j, nt[0] - 1), 0)

    gs = pltpu.PrefetchScalarGridSpec(
        num_scalar_prefetch=2,
        grid=(grid_n,),
        in_specs=[pl.BlockSpec((TME, d), row_map),
                  pl.BlockSpec((None, d, eh2), lambda j, te, nt: (te[j], 0, 0)),
                  pl.BlockSpec((None, eh, d), lambda j, te, nt: (te[j], 0, 0))],
        out_specs=pl.BlockSpec((TME, d), row_map),
    )
    return pl.pallas_call(
        functools.partial(_swiglu_kernel, eh),
        grid_spec=gs,
        out_shape=jax.ShapeDtypeStruct((r, d), out_dtype),
        compiler_params=_cparams(("arbitrary",)),
        name="grouped_swiglu",
    )(tile_expert, n_tiles, xs, w_in, w_out)


def _combine_kernel(tm, alpha, has_next, dest_ref, wgt_ref, sh_ref, h_ref, gate_ref, lng_ref, lnb_ref, sc_ref, sh2_ref,
                    ys_ref, *rest):
    if has_next:
        hn_ref, u_ref, buf_ref, sem = rest
    else:
        hn_ref, buf_ref, sem = rest

    def row_copy(t, kk):
        return pltpu.make_async_copy(ys_ref.at[pl.ds(dest_ref[0, t * TOP_K + kk], 1), :],
                                     buf_ref.at[kk, pl.ds(t, 1), :], sem)

    def issue(t, c):
        for kk in range(TOP_K):
            row_copy(t, kk).start()
        return c

    lax.fori_loop(0, tm, issue, 0)

    def drain(t, c):
        for kk in range(TOP_K):
            row_copy(t, kk).wait()
        return c

    lax.fori_loop(0, tm, drain, 0)

    wgt = wgt_ref[...]
    f = sh_ref[...]
    for kk in range(TOP_K):
        f = f + _lane_col(wgt, kk) * buf_ref[kk]
    y = _layer_norm_rows(alpha * h_ref[...] + gate_ref[...] * f, lng_ref[...], lnb_ref[...])
    hn_ref[...] = y
    if has_next:
        u_ref[...] = (y * (1.0 + sc_ref[...]) + sh2_ref[...]).astype(u_ref.dtype)


def moe_combine_ln(rows, ys, dest, wgt, shared, h, modv, layer, lng, lnb, alpha, has_next):
    tm = 128
    t, d = rows.t, rows.d
    vec = pl.BlockSpec((1, d), lambda i: (0, 0))
    nl = layer + 1 if has_next else layer
    out_specs = [rows.row_spec(tm)]
    out_shape = [jax.ShapeDtypeStruct((t, d), F32)]
    if has_next:
        out_specs.append(rows.row_spec(tm))
        out_shape.append(jax.ShapeDtypeStruct((t, d), BF16))
    return pl.pallas_call(
        functools.partial(_combine_kernel, tm, alpha, has_next),
        grid=(t // tm,),
        in_specs=[pl.BlockSpec((None, 1, tm * TOP_K), lambda i: (i, 0, 0), memory_space=pltpu.SMEM),
                  pl.BlockSpec((tm, LANES), lambda i: (i, 0)), rows.row_spec(tm), rows.row_spec(tm),
                  rows.mod_spec(layer, 5, tm), vec, vec, rows.mod_spec(nl, 1, tm), rows.mod_spec(nl, 0, tm),
                  pl.BlockSpec(memory_space=pl.ANY)],
        out_specs=out_specs,
        out_shape=out_shape,
        scratch_shapes=[pltpu.VMEM((TOP_K, tm, d), F32), pltpu.SemaphoreType.DMA],
        compiler_params=_cparams(("arbitrary",)),
        name="moe_combine_ln",
    )(dest.reshape(t // tm, 1, tm * TOP_K), wgt, shared, h, modv, lng.reshape(1, d), lnb.reshape(1, d), modv, modv, ys)


def moe_ffn(rows, v32, h, modv, layer, lng, lnb, alpha, has_next, router_w, router_bias, exp_w_in, exp_w_out,
            sh_w_in, sh_w_out):
    t, d = v32.shape
    ne = router_w.shape[1]
    eid, pos, wgt, cnt = router(v32, router_w, router_bias)
    counts = cnt[0].astype(I32)
    padded = (counts + TME - 1) // TME * TME
    ends = jnp.cumsum(padded)
    offs = ends - padded
    eid8 = eid[:, :TOP_K]
    dest = offs[eid8] + pos[:, :TOP_K]
    n_rows = (t * TOP_K // TME + ne) * TME
    tile_start = jnp.arange(n_rows // TME, dtype=I32) * TME
    tile_expert = jnp.minimum(jnp.sum((tile_start[:, None] >= ends[None, :]).astype(I32), axis=1), ne - 1)
    n_tiles = (ends[-1] // TME).reshape(1)

    xs = dispatch(v32, dest, n_rows)
    ys = grouped_swiglu(xs, exp_w_in.astype(BF16), exp_w_out.astype(BF16), tile_expert, n_tiles)
    shared = grouped_swiglu(v32, sh_w_in.astype(BF16)[None], sh_w_out.astype(BF16)[None],
                            jnp.zeros((t // TME,), I32), jnp.full((1,), t // TME, I32))
    return moe_combine_ln(rows, ys, dest, wgt, shared, h, modv, layer, lng, lnb, alpha, has_next)


def _pad_cols(w, n):
    return jnp.pad(w, ((0, 0), (0, n - w.shape[1])))


def _pad_row(v, n=LANES):
    v = v.reshape(1, -1).astype(F32)
    return jnp.pad(v, ((0, 0), (0, n - v.shape[1])))


def retention_mlstm_mixer(rows, order, u, w_in, w_out, ret_decay_rate, ret_norm_g, ml_gate_b, ml_norm_g):
    nh_r = ret_decay_rate.shape[-1]
    nh_m = ml_gate_b.shape[-1]
    ret_w = ret_norm_g.shape[0]
    ml_w = ml_norm_g.shape[0]
    hd_r, hd_m = ret_w // nh_r, ml_w // nh_m
    assert hd_r == hd_m
    main_w = 4 * ret_w + 4 * ml_w
    wb = w_in.astype(BF16)
    p = matmul(u, wb[:, :main_w], name="ab_in_proj")
    pg = matmul(u, _pad_cols(wb[:, main_w:], LANES), name="ab_gate_proj")

    pos = jnp.arange(rows.ctx + rows.seq, dtype=F32)
    half = hd_r // 2
    inv_freq = ROPE_BASE ** (-jnp.arange(half, dtype=F32) / half)
    ang = pos[:, None] * inv_freq[None, :]
    rate = jnp.broadcast_to(ret_decay_rate.astype(F32).reshape(2 * nh_r, 1, 1), (2 * nh_r, 1, LANES))
    y_r = retention(order, p, jnp.cos(ang), jnp.sin(ang), rate, ret_norm_g, nh_r, hd_r)

    gc, gr = ml_gates(pg, _pad_row(ml_gate_b), nh_m)
    y_m = mlstm(order, p, 4 * nh_r, gc, gr, ml_norm_g, nh_m, hd_m)
    return matmul2(y_r, y_m, w_out.astype(BF16), name="ab_out_proj")


def gated_deltanet_mixer(rows, order, u, w_in, conv_w, a_log, dt_bias, norm_g, w_out):
    nh = a_log.shape[-1]
    hd = norm_g.shape[0]
    dn_w = nh * hd
    wb = w_in.astype(BF16)
    p = matmul(u, wb[:, :4 * dn_w], name="dn_in_proj")
    pg = matmul(u, _pad_cols(wb[:, 4 * dn_w:], LANES), name="dn_gate_proj")
    conv_w9 = jnp.pad(conv_w.reshape(9, 3 * dn_w).astype(F32), ((0, 7), (0, 0)))
    qkv = dn_conv(rows, p, conv_w9, dn_w, hd)
    gc, gr, gt = dn_gates(pg, _pad_row(a_log), _pad_row(dt_bias), nh)
    y = gated_deltanet(order, qkv, p, 3 * (nh // DN_HG), gc, gr, gt, norm_g, nh, hd)
    return matmul(y, w_out.astype(BF16), name="dn_out_proj")


def kernel(x, c, ctx, c_ctx, ada_w, ada_b, ln_g, ln_b, ab_w_in, ab_w_out, ret_decay_rate, ret_norm_g, ml_gate_b, ml_norm_g, dn_w_in, dn_conv_w, dn_a_log, dn_dt_bias, dn_norm_g, dn_w_out, router_w, router_bias, exp_w_in, exp_w_out, sh_w_in, sh_w_out):
    b, seq, d = x.shape
    n_ctx = ctx.shape[1]
    depth = ada_w.shape[0]
    alpha = (2 * depth) ** 0.25
    rows = Rows(b, n_ctx, seq, d)
    order = ScanOrder(b, n_ctx, seq)
    assert b + 1 <= MOD_ROWS

    cvec = jnp.zeros((MOD_ROWS, d), F32).at[:b].set(c).at[b].set(c_ctx)
    modv = ada_table(cvec, ada_w, ada_b).reshape(depth * MOD_ROWS * 6, 1, d)

    h = jnp.concatenate([ctx.reshape(b * n_ctx, d), x.reshape(b * seq, d)], axis=0)
    u = modulate(rows, h, modv, 0, 1, 0)
    for layer in range(depth):
        j = layer // 2
        if layer % 2 == 0:
            z = retention_mlstm_mixer(rows, order, u, ab_w_in[j], ab_w_out[j], ret_decay_rate[j], ret_norm_g[j],
                                      ml_gate_b[j], ml_norm_g[j])
        else:
            z = gated_deltanet_mixer(rows, order, u, dn_w_in[j], dn_conv_w[j], dn_a_log[j], dn_dt_bias[j],
                                     dn_norm_g[j], dn_w_out[j])
        h, v32 = ln_residual(rows, h, z, modv, layer, 2, ln_g[layer, 0], ln_b[layer, 0], alpha, 4, 3, layer, F32)
        has_next = layer < depth - 1
        outs = moe_ffn(rows, v32, h, modv, layer, ln_g[layer, 1], ln_b[layer, 1], alpha, has_next, router_w[layer],
                       router_bias[layer], exp_w_in[layer], exp_w_out[layer], sh_w_in[layer], sh_w_out[layer])
        if has_next:
            h, u = outs
        else:
            h = outs[0]
    return h[b * n_ctx:].reshape(b, seq, d)
```

```python
import functools
import math

import jax
import jax.numpy as jnp
from jax import lax
from jax.experimental import pallas as pl
from jax.experimental.pallas import tpu as pltpu

F32 = jnp.float32
BF16 = jnp.bfloat16
I32 = jnp.int32
U32 = jnp.uint32

V7X_VMEM_BYTES = 64 * 1024 * 1024
VMEM_LIMIT = V7X_VMEM_BYTES - 8 * 1024 * 1024
LANES = 128

GRID_W = 64
CHUNK = 64
TOP_K = 8
N_GROUPS = 8
TOPK_GROUPS = 4
ROUTED_SCALE = 2.5
ROPE_BASE = 10000.0
LN_EPS = 1e-5

LB = 256
TME = 256
MOD_ROWS = 8


def _cparams(sem):
    return pltpu.CompilerParams(dimension_semantics=sem, vmem_limit_bytes=VMEM_LIMIT)


def _bdot(a, b):
    return jnp.dot(a.astype(BF16), b.astype(BF16), preferred_element_type=F32)


def _bdot_nt(a, b):
    return lax.dot_general(a.astype(BF16), b.astype(BF16), (((1,), (1,)), ((), ())), preferred_element_type=F32)


def _bdot_tn(a, b):
    return lax.dot_general(a.astype(BF16), b.astype(BF16), (((0,), (0,)), ((), ())), preferred_element_type=F32)


def _fdot(a, b):
    return jnp.dot(a, b, preferred_element_type=F32, precision=lax.Precision.HIGHEST)


def _sigmoid(x):
    return 1.0 / (1.0 + jnp.exp(-x))


def _silu(x):
    return x * _sigmoid(x)


def _softplus(x):
    return jnp.maximum(x, 0.0) + jnp.log(1.0 + jnp.exp(-jnp.abs(x)))


def _log_sigmoid(x):
    return -_softplus(-x)


def _lane_col(x, lane):
    idx = lax.broadcasted_iota(I32, x.shape, 1)
    return jnp.sum(jnp.where(idx == lane, x, 0.0), axis=1, keepdims=True)


def _ada_kernel(c_ref, w_ref, b_ref, o_ref):
    o_ref[...] = _bdot(_silu(c_ref[...]), w_ref[...]) + b_ref[...]


def ada_table(cvec, ada_w, ada_b):
    depth, d, n = ada_w.shape
    tn = 512
    return pl.pallas_call(
        _ada_kernel,
        grid=(depth, n // tn),
        in_specs=[pl.BlockSpec((MOD_ROWS, d), lambda l, j: (0, 0)),
                  pl.BlockSpec((None, d, tn), lambda l, j: (l, 0, j)),
                  pl.BlockSpec((None, 1, tn), lambda l, j: (l, 0, j))],
        out_specs=pl.BlockSpec((None, MOD_ROWS, tn), lambda l, j: (l, 0, j)),
        out_shape=jax.ShapeDtypeStruct((depth, MOD_ROWS, n), F32),
        compiler_params=_cparams(("parallel", "parallel")),
        name="ada_table",
    )(cvec, ada_w, ada_b.reshape(depth, 1, n))


class Rows:
    def __init__(self, b, ctx, seq, d):
        self.b, self.ctx, self.seq, self.d = b, ctx, seq, d
        self.n_ctx = b * ctx
        self.t = b * ctx + b * seq
        self.tm = math.gcd(math.gcd(self.n_ctx, seq), 512)

    def seg(self, i, tm):
        n_ctx_tiles = self.n_ctx // tm
        return jnp.where(i < n_ctx_tiles, self.b, (i - n_ctx_tiles) // (self.seq // tm))

    def mod_spec(self, layer, which, tm):
        return pl.BlockSpec((None, 1, self.d), lambda i: ((layer * MOD_ROWS + self.seg(i, tm)) * 6 + which, 0, 0))

    def row_spec(self, tm, width=None):
        return pl.BlockSpec((tm, self.d if width is None else width), lambda i: (i, 0))


def _mod_kernel(h_ref, sc_ref, sh_ref, o_ref):
    o_ref[...] = (h_ref[...] * (1.0 + sc_ref[...]) + sh_ref[...]).astype(o_ref.dtype)


def modulate(rows, h, modv, layer, sc_idx, sh_idx):
    tm = rows.tm
    return pl.pallas_call(
        _mod_kernel,
        grid=(rows.t // tm,),
        in_specs=[rows.row_spec(tm), rows.mod_spec(layer, sc_idx, tm), rows.mod_spec(layer, sh_idx, tm)],
        out_specs=rows.row_spec(tm),
        out_shape=jax.ShapeDtypeStruct((rows.t, rows.d), BF16),
        compiler_params=_cparams(("parallel",)),
        name="modulate",
    )(h, modv, modv)


def _layer_norm_rows(r, g, b):
    mu = jnp.mean(r, axis=-1, keepdims=True)
    xc = r - mu
    var = jnp.mean(xc * xc, axis=-1, keepdims=True)
    return xc * lax.rsqrt(var + LN_EPS) * g + b


def _mm_kernel(x_ref, w_ref, o_ref):
    o_ref[...] = _bdot(x_ref[...], w_ref[...]).astype(o_ref.dtype)


def _pick_tile(n, prefs):
    for p in prefs:
        if n % p == 0:
            return p
    return n


def matmul(x, w, out_dtype=F32, name="matmul"):
    m, k = x.shape
    n = w.shape[1]
    tm = _pick_tile(m, (768, 512, 256, 128))
    tn = _pick_tile(n, (1024, 512, 256, 128))
    return pl.pallas_call(
        _mm_kernel,
        grid=(n // tn, m // tm),
        in_specs=[pl.BlockSpec((tm, k), lambda j, i: (i, 0)), pl.BlockSpec((k, tn), lambda j, i: (0, j))],
        out_specs=pl.BlockSpec((tm, tn), lambda j, i: (i, j)),
        out_shape=jax.ShapeDtypeStruct((m, n), out_dtype),
        compiler_params=_cparams(("parallel", "parallel")),
        name=name,
    )(x, w)


def _mm2_kernel(x1_ref, x2_ref, w1_ref, w2_ref, o_ref):
    o_ref[...] = _bdot(x1_ref[...], w1_ref[...]) + _bdot(x2_ref[...], w2_ref[...])


def matmul2(x1, x2, w, name="matmul2"):
    m, k1 = x1.shape
    k2 = x2.shape[1]
    n = w.shape[1]
    tm = _pick_tile(m, (768, 512, 256, 128))
    tn = _pick_tile(n, (1024, 512, 256, 128))
    return pl.pallas_call(
        _mm2_kernel,
        grid=(n // tn, m // tm),
        in_specs=[pl.BlockSpec((tm, k1), lambda j, i: (i, 0)), pl.BlockSpec((tm, k2), lambda j, i: (i, 0)),
                  pl.BlockSpec((k1, tn), lambda j, i: (0, j)), pl.BlockSpec((k2, tn), lambda j, i: (k1 // k2, j))],
        out_specs=pl.BlockSpec((tm, tn), lambda j, i: (i, j)),
        out_shape=jax.ShapeDtypeStruct((m, n), F32),
        compiler_params=_cparams(("parallel", "parallel")),
        name=name,
    )(x1, x2, w, w)


class ScanOrder:
    def __init__(self, b, ctx, seq):
        assert ctx % LB == 0 and seq % LB == 0
        self.b = b
        self.nc = ctx // LB
        self.nl = seq // LB
        self.steps = self.nc + self.nl

    def row_block(self, direction, bi, s):
        nc, nl = self.nc, self.nl
        if direction == 0:
            c, l = s, s - nc
        else:
            c, l = nc - 1 - s, nl - 1 - (s - nc)
        return jnp.where(s < nc, bi * nc + c, self.b * nc + bi * nl + l)

    def pos_block(self, direction, s):
        nc, nl = self.nc, self.nl
        if direction == 0:
            return s
        return jnp.where(s < nc, nc - 1 - s, nc + nl - 1 - (s - nc))


def _causal(direction, n):
    ri = lax.broadcasted_iota(I32, (n, n), 0)
    ci = lax.broadcasted_iota(I32, (n, n), 1)
    rel = ri - ci if direction == 0 else ci - ri
    return rel >= 0, rel > 0, rel


def _rope(t, cos, sin):
    half = t.shape[-1] // 2
    t1, t2 = t[:, :half], t[:, half:]
    return jnp.concatenate([t1 * cos - t2 * sin, t1 * sin + t2 * cos], axis=-1)


def _head_ln_gate(o, g, gate_act):
    mu = jnp.mean(o, axis=-1, keepdims=True)
    xc = o - mu
    var = jnp.mean(xc * xc, axis=-1, keepdims=True)
    return xc * lax.rsqrt(var + LN_EPS) * g * gate_act


def _ret_kernel(direction, hd, *refs):
    if direction == 0:
        q_ref, k_ref, v_ref, cos_ref, sin_ref, rate_ref, o_ref, s_ref = refs
    else:
        q_ref, k_ref, v_ref, cos_ref, sin_ref, rate_ref, of_ref, g_ref, ng_ref, o_ref, s_ref = refs
    step = pl.program_id(2)

    @pl.when(step == 0)
    def _():
        s_ref[...] = jnp.zeros_like(s_ref)

    lg = -jnp.exp(rate_ref[...])[:, :1]
    incl, _, rel = _causal(direction, LB)
    ri = lax.broadcasted_iota(I32, (LB, 1), 0).astype(F32)
    if direction == 0:
        q_dec = jnp.exp(lg * (ri + 1.0))
        k_dec = jnp.exp(lg * (LB - 1.0 - ri))
    else:
        q_dec = jnp.exp(lg * (LB - ri))
        k_dec = jnp.exp(lg * ri)
    intra = jnp.where(incl, jnp.exp(lg * jnp.maximum(rel, 0).astype(F32)), 0.0)
    c_dec = jnp.exp(lg * LB)

    cos, sin = cos_ref[...], sin_ref[...]
    q = _rope(q_ref[...], cos, sin)
    k = _rope(k_ref[...], cos, sin) * hd ** -0.5
    v = v_ref[...]
    s = s_ref[...]
    att = _bdot_nt(q, k) * intra
    o = _bdot(att, v) + _bdot(q * q_dec, s)
    s_ref[...] = s * c_dec + _bdot_tn(k * k_dec, v)
    if direction == 0:
        o_ref[...] = o
    else:
        o_ref[...] = _head_ln_gate(o + of_ref[...], ng_ref[...], _silu(g_ref[...])).astype(o_ref.dtype)


def retention(order, p, cos, sin, rate, norm_g, nh, hd):
    t = p.shape[0]
    b = order.b
    half = hd // 2

    def run(direction, extra_in, extra_specs, out_dtype):
        def blk(col0):
            return pl.BlockSpec((LB, hd), lambda bi, h, s: (order.row_block(direction, bi, s), col0 + h))
        in_specs = [blk(0), blk(nh), blk(2 * nh),
                    pl.BlockSpec((LB, half), lambda bi, h, s: (order.pos_block(direction, s), 0)),
                    pl.BlockSpec((LB, half), lambda bi, h, s: (order.pos_block(direction, s), 0)),
                    pl.BlockSpec((None, 1, LANES), lambda bi, h, s: (direction * nh + h, 0, 0))] + extra_specs
        return pl.pallas_call(
            functools.partial(_ret_kernel, direction, hd),
            grid=(b, nh, order.steps),
            in_specs=in_specs,
            out_specs=pl.BlockSpec((LB, hd), lambda bi, h, s: (order.row_block(direction, bi, s), h)),
            out_shape=jax.ShapeDtypeStruct((t, nh * hd), out_dtype),
            scratch_shapes=[pltpu.VMEM((hd, hd), F32)],
            compiler_params=_cparams(("parallel", "parallel", "arbitrary")),
            name=f"retention_d{direction}",
        )(p, p, p, cos, sin, rate, *extra_in)

    o_fwd = run(0, [], [], F32)
    return run(1, [o_fwd, p, norm_g.reshape(1, nh * hd)],
               [pl.BlockSpec((LB, hd), lambda bi, h, s: (order.row_block(1, bi, s), h)),
                pl.BlockSpec((LB, hd), lambda bi, h, s: (order.row_block(1, bi, s), 3 * nh + h)),
                pl.BlockSpec((1, hd), lambda bi, h, s: (0, h))], BF16)


def _ml_gate_kernel(nh, g_ref, b_ref, gc_ref, gr_ref):
    g = g_ref[...] + b_ref[...]
    ls = _log_sigmoid(g)
    incl_f, _, _ = _causal(0, LB)
    incl_b, _, _ = _causal(1, LB)
    pre = _fdot(incl_f.astype(F32), ls)
    suf = _fdot(incl_b.astype(F32), ls)
    lane = lax.broadcasted_iota(I32, g.shape, 1)
    is_f = (lane // nh) % 2 == 1
    is_d1 = lane // (2 * nh) == 1
    out = jnp.where(is_f, jnp.where(is_d1, suf, pre), g)
    gc_ref[...] = out
    gr_ref[...] = out.T


def ml_gates(pg, bias_row, nh):
    t = pg.shape[0]
    return pl.pallas_call(
        functools.partial(_ml_gate_kernel, nh),
        grid=(t // LB,),
        in_specs=[pl.BlockSpec((LB, LANES), lambda i: (i, 0)), pl.BlockSpec((1, LANES), lambda i: (0, 0))],
        out_specs=[pl.BlockSpec((LB, LANES), lambda i: (i, 0)), pl.BlockSpec((None, LANES, LB), lambda i: (i, 0, 0))],
        out_shape=[jax.ShapeDtypeStruct((t, LANES), F32), jax.ShapeDtypeStruct((t // LB, LANES, LB), F32)],
        compiler_params=_cparams(("parallel",)),
        name="ml_gates",
    )(pg, bias_row)


def _ml_kernel(direction, nh, hd, *refs):
    if direction == 0:
        q_ref, k_ref, v_ref, gc_ref, gr_ref, o_ref, c_ref, n_ref, m_ref = refs
    else:
        q_ref, k_ref, v_ref, gc_ref, gr_ref, of_ref, g_ref, ng_ref, o_ref, c_ref, n_ref, m_ref = refs
    h = pl.program_id(1)
    step = pl.program_id(2)

    @pl.when(step == 0)
    def _():
        c_ref[...] = jnp.zeros_like(c_ref)
        n_ref[...] = jnp.zeros_like(n_ref)
        m_ref[...] = jnp.zeros_like(m_ref)

    lane_i = direction * 2 * nh + h
    lane_f = lane_i + nh
    gc = gc_ref[...]
    ig_col = _lane_col(gc, lane_i)
    b_col = _lane_col(gc, lane_f)
    ig_row = gr_ref[pl.ds(lane_i, 1), :]
    b_row = gr_ref[pl.ds(lane_f, 1), :]
    m_prev = m_ref[0:1, 0:1]

    incl, _, _ = _causal(direction, LB)
    d_log = jnp.where(incl, b_col - b_row + ig_row, -jnp.inf)
    inter_log = b_col + m_prev
    m_row = jnp.maximum(inter_log, jnp.max(d_log, axis=1, keepdims=True))
    d_w = jnp.exp(d_log - m_row)
    inter_w = jnp.exp(inter_log - m_row)

    q = q_ref[...]
    k = k_ref[...] * hd ** -0.5
    v = v_ref[...]
    c_mat = c_ref[...]
    n_row = n_ref[0:1, :]
    s = _bdot_nt(q, k) * d_w
    num = _bdot(s, v) + inter_w * _bdot(q, c_mat)
    den = jnp.sum(s, axis=1, keepdims=True) + inter_w * jnp.sum(q * n_row, axis=1, keepdims=True)
    hout = num / jnp.maximum(jnp.abs(den), jnp.exp(-m_row))

    last = LB - 1 if direction == 0 else 0
    b_last = b_col[last:last + 1, :]
    m_last = m_row[last:last + 1, :]
    dec_last = inter_w[last:last + 1, :]
    w_last = jnp.exp(b_last - b_col + ig_col - m_last)
    kw = k * w_last
    c_ref[...] = dec_last * c_mat + _bdot_tn(kw, v)
    n_ref[0:1, :] = dec_last * n_row + jnp.sum(kw, axis=0, keepdims=True)
    m_ref[...] = jnp.broadcast_to(m_last, m_ref.shape)
    if direction == 0:
        o_ref[...] = hout
    else:
        o_ref[...] = _head_ln_gate(hout + of_ref[...], ng_ref[...], _sigmoid(g_ref[...])).astype(o_ref.dtype)


def mlstm(order, p, col0, gc, gr, norm_g, nh, hd):
    t = p.shape[0]
    b = order.b

    def run(direction, extra_in, extra_specs, out_dtype):
        def blk(c):
            return pl.BlockSpec((LB, hd), lambda bi, h, s: (order.row_block(direction, bi, s), col0 + c + h))
        in_specs = [blk(0), blk(nh), blk(2 * nh),
                    pl.BlockSpec((LB, LANES), lambda bi, h, s: (order.row_block(direction, bi, s), 0)),
                    pl.BlockSpec((None, LANES, LB), lambda bi, h, s: (order.row_block(direction, bi, s), 0, 0))]
        return pl.pallas_call(
            functools.partial(_ml_kernel, direction, nh, hd),
            grid=(b, nh, order.steps),
            in_specs=in_specs + extra_specs,
            out_specs=pl.BlockSpec((LB, hd), lambda bi, h, s: (order.row_block(direction, bi, s), h)),
            out_shape=jax.ShapeDtypeStruct((t, nh * hd), out_dtype),
            scratch_shapes=[pltpu.VMEM((hd, hd), F32), pltpu.VMEM((8, hd), F32), pltpu.VMEM((8, LANES), F32)],
            compiler_params=_cparams(("parallel", "parallel", "arbitrary")),
            name=f"mlstm_d{direction}",
        )(p, p, p, gc, gr, *extra_in)

    o_fwd = run(0, [], [], F32)
    return run(1, [o_fwd, p, norm_g.reshape(1, nh * hd)],
               [pl.BlockSpec((LB, hd), lambda bi, h, s: (order.row_block(1, bi, s), h)),
                pl.BlockSpec((LB, hd), lambda bi, h, s: (order.row_block(1, bi, s), col0 + 3 * nh + h)),
                pl.BlockSpec((1, hd), lambda bi, h, s: (0, h))], BF16)


CONV_HALO = 128


def _conv_kernel(tb, tc, n_ctx, ctx, seq, n_qk_tiles, n_q_tiles, hd, xp_ref, xc_ref, xn_ref, w_ref, o_ref, buf_ref):
    i = pl.program_id(0)
    j = pl.program_id(1)
    buf_ref[0:CONV_HALO, :] = xp_ref[...]
    buf_ref[CONV_HALO:CONV_HALO + tb, :] = xc_ref[...]
    buf_ref[CONV_HALO + tb:, :] = xn_ref[...]

    row = i * tb + lax.broadcasted_iota(I32, (tb, 1), 0)
    is_ctx = i * tb < n_ctx
    li = jnp.where(is_ctx, row % ctx, (row - n_ctx) % seq)
    rr = jnp.where(is_ctx, 0, li // GRID_W)
    cc = jnp.where(is_ctx, li, li % GRID_W)
    n_r = jnp.where(is_ctx, 1, seq // GRID_W)
    n_c = jnp.where(is_ctx, ctx, GRID_W)

    acc = jnp.zeros((tb, tc), F32)
    for dr in range(3):
        for dc in range(3):
            off = (dr - 1) * GRID_W + (dc - 1)
            r2 = rr + (dr - 1)
            c2 = cc + (dc - 1)
            ok = (r2 >= 0) & (r2 < n_r) & (c2 >= 0) & (c2 < n_c)
            src = buf_ref[CONV_HALO + off:CONV_HALO + off + tb, :]
            acc = acc + jnp.where(ok, src, 0.0) * w_ref[dr * 3 + dc:dr * 3 + dc + 1, :]
    y = _silu(acc)
    qscale = jnp.where(j < n_q_tiles, hd ** -0.5, 1.0)
    is_qk = j < n_qk_tiles
    for g in range(tc // hd):
        yg = y[:, g * hd:(g + 1) * hd]
        nrm = lax.rsqrt(jnp.sum(yg * yg, axis=1, keepdims=True) + 1e-6) * qscale
        o_ref[:, g * hd:(g + 1) * hd] = yg * jnp.where(is_qk, nrm, 1.0)


def dn_conv(rows, p, conv_w9, dn_w, hd):
    t = rows.t
    tb = rows.tm
    tc = 512
    hb = tb // CONV_HALO
    n_halo = t // CONV_HALO
    kern = functools.partial(_conv_kernel, tb, tc, rows.n_ctx, rows.ctx, rows.seq, 2 * dn_w // tc, dn_w // tc, hd)
    return pl.pallas_call(
        kern,
        grid=(t // tb, 3 * dn_w // tc),
        in_specs=[pl.BlockSpec((CONV_HALO, tc), lambda i, j: (jnp.maximum(i * hb - 1, 0), j)),
                  pl.BlockSpec((tb, tc), lambda i, j: (i, j)),
                  pl.BlockSpec((CONV_HALO, tc), lambda i, j: (jnp.minimum(i * hb + hb, n_halo - 1), j)),
                  pl.BlockSpec((16, tc), lambda i, j: (0, j))],
        out_specs=pl.BlockSpec((tb, tc), lambda i, j: (i, j)),
        out_shape=jax.ShapeDtypeStruct((t, 3 * dn_w), F32),
        scratch_shapes=[pltpu.VMEM((tb + 2 * CONV_HALO, tc), F32)],
        compiler_params=_cparams(("parallel", "parallel")),
        name="dn_conv",
    )(p, p, p, conv_w9)


def _chunk_masks(direction):
    ri = lax.broadcasted_iota(I32, (LB, LB), 0)
    ci = lax.broadcasted_iota(I32, (LB, LB), 1)
    same = (ri // CHUNK) == (ci // CHUNK)
    rel = ri - ci if direction == 0 else ci - ri
    return same & (rel >= 0), same & (rel > 0), same


def _dn_gate_kernel(nh, ab_ref, alog_ref, dtb_ref, gc_ref, gr_ref, gt_ref):
    x = ab_ref[...]
    lane = lax.broadcasted_iota(I32, x.shape, 1)
    g = -jnp.exp(alog_ref[...]) * _softplus(x + dtb_ref[...])
    g = jnp.where(lane < 2 * nh, g, 0.0)
    incl_f, _, same = _chunk_masks(0)
    incl_b, _, _ = _chunk_masks(1)
    pre = _fdot(incl_f.astype(F32), g)
    suf = _fdot(incl_b.astype(F32), g)
    tot = _fdot(same.astype(F32), g)
    out = jnp.where(lane < nh, pre, jnp.where(lane < 2 * nh, suf, _sigmoid(x)))
    gc_ref[...] = out
    gr_ref[...] = out.T
    gt_ref[...] = tot


def dn_gates(pg, alog_row, dtb_row, nh):
    t = pg.shape[0]
    return pl.pallas_call(
        functools.partial(_dn_gate_kernel, nh),
        grid=(t // LB,),
        in_specs=[pl.BlockSpec((LB, LANES), lambda i: (i, 0)), pl.BlockSpec((1, LANES), lambda i: (0, 0)),
                  pl.BlockSpec((1, LANES), lambda i: (0, 0))],
        out_specs=[pl.BlockSpec((LB, LANES), lambda i: (i, 0)), pl.BlockSpec((None, LANES, LB), lambda i: (i, 0, 0)),
                   pl.BlockSpec((LB, LANES), lambda i: (i, 0))],
        out_shape=[jax.ShapeDtypeStruct((t, LANES), F32), jax.ShapeDtypeStruct((t // LB, LANES, LB), F32),
                   jax.ShapeDtypeStruct((t, LANES), F32)],
        compiler_params=_cparams(("parallel",)),
        name="dn_gates",
    )(pg, alog_row, dtb_row)


DN_HG = 4
DN_SCAN_HG = 8


def _dn_ut_kernel(nh, hd, q_ref, k_ref, v_ref, gc_ref, gr_ref, gt_ref, *refs):
    outs = (refs[0:5], refs[5:10])
    a_ref, t_ref, x_ref, rhs_ref = refs[10:]
    hg = pl.program_id(1)
    ri = lax.broadcasted_iota(I32, (LB, LB), 0)
    ci = lax.broadcasted_iota(I32, (LB, LB), 1)
    eye = (ri == ci).astype(F32)
    same = (ri // CHUNK) == (ci // CHUNK)
    gcb = gc_ref[...]
    gtb = gt_ref[...]
    chains = [(d, hh) for d in (0, 1) for hh in range(DN_HG)]
    for c, (d, hh) in enumerate(chains):
        u_ref, w_ref, qg_ref, kd_ref, att_ref = outs[d]
        rel = ri - ci if d == 0 else ci - ri
        lane_g = d * nh + hg * DN_HG + hh
        gc_col = _lane_col(gcb, lane_g)
        beta = _lane_col(gcb, 2 * nh + lane_g)
        gt_col = _lane_col(gtb, lane_g)
        gc_row = gr_ref[pl.ds(lane_g, 1), :]
        sl = slice(hh * hd, (hh + 1) * hd)
        q, k, v = q_ref[:, sl], k_ref[:, sl], v_ref[:, sl]
        decay = jnp.exp(jnp.where(same & (rel >= 0), gc_col - gc_row, -jnp.inf))
        kb = k * beta
        a = jnp.where(rel > 0, _bdot_nt(kb, k) * decay, 0.0)
        a_ref[c] = a
        t_ref[c] = eye - jnp.where((ri // 2) == (ci // 2), a, 0.0)
        eg = jnp.exp(gc_col)
        rhs_ref[c] = jnp.concatenate([v * beta, kb * eg], axis=1).astype(rhs_ref.dtype)
        qg_ref[:, sl] = (q * eg).astype(qg_ref.dtype)
        kd_ref[:, sl] = (k * jnp.exp(gt_col - gc_col)).astype(kd_ref.dtype)
        att_ref[:, hh * LB:(hh + 1) * LB] = (_bdot_nt(q, k) * decay).astype(att_ref.dtype)
    s = 2
    while s < CHUNK:
        m = ((ri // (2 * s)) == (ci // (2 * s))) & ((ri // s) != (ci // s))
        for c in range(len(chains)):
            x_ref[c] = _bdot(t_ref[c], jnp.where(m, a_ref[c], 0.0)).astype(x_ref.dtype)
        for c in range(len(chains)):
            t = t_ref[c]
            t_ref[c] = t - _bdot(x_ref[c], t)
        s *= 2
    for c, (d, hh) in enumerate(chains):
        u_ref, w_ref = outs[d][0], outs[d][1]
        sl = slice(hh * hd, (hh + 1) * hd)
        sol = _bdot(t_ref[c], rhs_ref[c])
        u_ref[:, sl] = sol[:, :hd]
        w_ref[:, sl] = sol[:, hd:].astype(w_ref.dtype)


def _dn_scan_kernel(direction, nh, hd, hgs, *refs):
    if direction == 0:
        u_ref, w_ref, qg_ref, kd_ref, att_ref, gt_ref, o_ref, s_ref, vn_ref = refs
    else:
        u_ref, w_ref, qg_ref, kd_ref, att_ref, gt_ref, of_ref, z_ref, ng_ref, o_ref, s_ref, vn_ref = refs
    hg = pl.program_id(1)
    step = pl.program_id(2)

    @pl.when(step == 0)
    def _():
        s_ref[...] = jnp.zeros_like(s_ref)

    vn_ref[...] = jnp.zeros_like(vn_ref)
    gtb = gt_ref[...]
    egs = [jnp.exp(_lane_col(gtb, direction * nh + hg * hgs + hh)) for hh in range(hgs)]
    n_chunks = LB // CHUNK
    for ci in range(n_chunks):
        cc = ci if direction == 0 else n_chunks - 1 - ci
        rs = slice(cc * CHUNK, (cc + 1) * CHUNK)
        for hh in range(hgs):
            sl = slice(hh * hd, (hh + 1) * hd)
            s = s_ref[hh]
            v_new = u_ref[rs, sl] - _bdot(w_ref[rs, sl], s)
            vn_ref[hh, rs, :] = v_new
            s_ref[hh] = egs[hh][cc * CHUNK:cc * CHUNK + 1, :] * s + _bdot_tn(kd_ref[rs, sl], v_new)
            o = _bdot(qg_ref[rs, sl], s) + _bdot(att_ref[rs, hh * LB:(hh + 1) * LB], vn_ref[hh])
            if direction == 0:
                o_ref[rs, sl] = o
            else:
                o = o + of_ref[rs, sl]
                rms = lax.rsqrt(jnp.mean(o * o, axis=1, keepdims=True) + 1e-6)
                o_ref[rs, sl] = (o * rms * ng_ref[...] * _silu(z_ref[rs, sl])).astype(o_ref.dtype)


def gated_deltanet(order, qkv, p, gc, gr, gt, norm_g, nh, hd):
    t = qkv.shape[0]
    b = order.b
    w = nh * hd
    gw = DN_HG * hd
    ng = nh // DN_HG
    nchain = 2 * DN_HG

    per_dir_specs = [pl.BlockSpec((LB, gw), lambda i, g: (i, g))] * 4 + [pl.BlockSpec((LB, DN_HG * LB), lambda i, g: (i, g))]
    per_dir_shape = ([jax.ShapeDtypeStruct((t, w), F32)] + [jax.ShapeDtypeStruct((t, w), BF16)] * 3
                     + [jax.ShapeDtypeStruct((t, nh * LB), BF16)])
    ut = pl.pallas_call(
        functools.partial(_dn_ut_kernel, nh, hd),
        grid=(t // LB, ng),
        in_specs=[pl.BlockSpec((LB, gw), lambda i, g: (i, g)), pl.BlockSpec((LB, gw), lambda i, g: (i, ng + g)),
                  pl.BlockSpec((LB, gw), lambda i, g: (i, 2 * ng + g)),
                  pl.BlockSpec((LB, LANES), lambda i, g: (i, 0)),
                  pl.BlockSpec((None, LANES, LB), lambda i, g: (i, 0, 0)),
                  pl.BlockSpec((LB, LANES), lambda i, g: (i, 0))],
        out_specs=per_dir_specs * 2,
        out_shape=per_dir_shape * 2,
        scratch_shapes=[pltpu.VMEM((nchain, LB, LB), F32), pltpu.VMEM((nchain, LB, LB), F32),
                        pltpu.VMEM((nchain, LB, LB), BF16), pltpu.VMEM((nchain, LB, 2 * hd), BF16)],
        compiler_params=_cparams(("parallel", "parallel")),
        name="dn_ut",
    )(qkv, qkv, qkv, gc, gr, gt)
    outs = (ut[0:5], ut[5:10])

    hgs = min(DN_SCAN_HG, nh)
    sw = hgs * hd
    nsg = nh // hgs

    def run(direction, extra_in, extra_specs, out_dtype):
        def blk(width):
            return pl.BlockSpec((LB, width), lambda bi, g, s: (order.row_block(direction, bi, s), g))
        return pl.pallas_call(
            functools.partial(_dn_scan_kernel, direction, nh, hd, hgs),
            grid=(b, nsg, order.steps),
            in_specs=[blk(sw)] * 4 + [blk(hgs * LB),
                      pl.BlockSpec((LB, LANES), lambda bi, g, s: (order.row_block(direction, bi, s), 0))] + extra_specs,
            out_specs=blk(sw),
            out_shape=jax.ShapeDtypeStruct((t, w), out_dtype),
            scratch_shapes=[pltpu.VMEM((hgs, hd, hd), F32), pltpu.VMEM((hgs, LB, hd), F32)],
            compiler_params=_cparams(("parallel", "parallel", "arbitrary")),
            name=f"dn_scan_d{direction}",
        )(*outs[direction], gt, *extra_in)

    o_fwd = run(0, [], [], F32)
    return run(1, [o_fwd, p, norm_g.reshape(1, hd)],
               [pl.BlockSpec((LB, sw), lambda bi, g, s: (order.row_block(1, bi, s), g)),
                pl.BlockSpec((LB, sw), lambda bi, g, s: (order.row_block(1, bi, s), 3 * nsg + g)),
                pl.BlockSpec((1, hd), lambda bi, g, s: (0, 0))], BF16)


def _pack_pairs(x):
    half = x.shape[1] // 2
    lo = pltpu.bitcast(x[:, :half].astype(BF16).astype(F32), U32) >> 16
    hi = pltpu.bitcast(x[:, half:].astype(BF16).astype(F32), U32) & jnp.uint32(0xFFFF0000)
    return hi | lo


def _unpack_pairs(w):
    lo = pltpu.bitcast(w << 16, F32)
    hi = pltpu.bitcast(w & jnp.uint32(0xFFFF0000), F32)
    return jnp.concatenate([lo, hi], axis=1)


def _ln_pack_kernel(alpha, h_ref, z_ref, gate_ref, lng_ref, lnb_ref, sc_ref, sh_ref, hn_ref, vp_ref):
    y = _layer_norm_rows(alpha * h_ref[...] + gate_ref[...] * z_ref[...], lng_ref[...], lnb_ref[...])
    hn_ref[...] = y
    vp_ref[...] = _pack_pairs(y * (1.0 + sc_ref[...]) + sh_ref[...])


def ln_residual_pack(rows, h, z, modv, layer, lng, lnb, alpha):
    tm = rows.tm // 2
    d = rows.d
    vec = pl.BlockSpec((1, d), lambda i: (0, 0))
    return pl.pallas_call(
        functools.partial(_ln_pack_kernel, alpha),
        grid=(rows.t // tm,),
        in_specs=[rows.row_spec(tm), rows.row_spec(tm), rows.mod_spec(layer, 2, tm), vec, vec,
                  rows.mod_spec(layer, 4, tm), rows.mod_spec(layer, 3, tm)],
        out_specs=[rows.row_spec(tm), rows.row_spec(tm, d // 2)],
        out_shape=[jax.ShapeDtypeStruct((rows.t, d), F32), jax.ShapeDtypeStruct((rows.t, d // 2), U32)],
        compiler_params=_cparams(("parallel",)),
        name="ln_residual_pack",
    )(h, z, modv, lng.reshape(1, d), lnb.reshape(1, d), modv, modv)


def _router_kernel(tm, ne, h_ref, sc_ref, sh_ref, w_ref, bias_ref, eid_ref, pos_ref, wgt_ref, cnt_ref, run_ref):
    i = pl.program_id(0)

    @pl.when(i == 0)
    def _():
        run_ref[...] = jnp.zeros_like(run_ref)

    x = h_ref[...] * (1.0 + sc_ref[...]) + sh_ref[...]
    scores = _sigmoid(_fdot(x, w_ref[...]))
    biased = scores + bias_ref[...]
    lane = lax.broadcasted_iota(I32, (tm, ne), 1)
    gsz = ne // N_GROUPS
    grp = lane // gsz
    neg = -jnp.inf

    gs = []
    for g in range(N_GROUPS):
        m = jnp.where(grp == g, biased, neg)
        top1 = jnp.max(m, axis=1, keepdims=True)
        n_top = jnp.sum((m == top1).astype(F32), axis=1, keepdims=True)
        rest = jnp.max(jnp.where(m < top1, m, neg), axis=1, keepdims=True)
        gs.append(top1 + jnp.where(n_top >= 2.0, top1, rest))
    emask = jnp.zeros((tm, ne), jnp.bool_)
    for g in range(N_GROUPS):
        beaten = jnp.zeros((tm, 1), F32)
        for g2 in range(N_GROUPS):
            if g2 == g:
                continue
            wins = (gs[g2] > gs[g]) | (gs[g2] == gs[g]) if g2 < g else gs[g2] > gs[g]
            beaten = beaten + wins.astype(F32)
        emask = emask | ((grp == g) & (beaten < TOPK_GROUPS))
    masked = jnp.where(emask, biased, neg)

    sel = jnp.zeros((tm, ne), jnp.bool_)
    idxs = []
    lane_f = lane.astype(F32)
    for _ in range(TOP_K):
        mx = jnp.max(masked, axis=1, keepdims=True)
        idx = jnp.min(jnp.where(masked == mx, lane_f, float(ne)), axis=1, keepdims=True).astype(I32)
        hit = lane == idx
        sel = sel | hit
        masked = jnp.where(hit, neg, masked)
        idxs.append(idx)
    w_sel = jnp.where(sel, scores, 0.0)
    w_sel = w_sel / jnp.sum(w_sel, axis=1, keepdims=True) * ROUTED_SCALE

    ri = lax.broadcasted_iota(I32, (tm, tm), 0)
    ci = lax.broadcasted_iota(I32, (tm, tm), 1)
    self = sel.astype(F32)
    before = _bdot((ri > ci).astype(F32), self) + run_ref[0:1, :]
    run_ref[0:1, :] = run_ref[0:1, :] + jnp.sum(self, axis=0, keepdims=True)
    cnt_ref[...] = jnp.broadcast_to(run_ref[0:1, :], cnt_ref.shape)

    lane_o = lax.broadcasted_iota(I32, (tm, LANES), 1)
    eid = jnp.zeros((tm, LANES), I32)
    pos = jnp.zeros((tm, LANES), I32)
    wgt = jnp.zeros((tm, LANES), F32)
    for kk, idx in enumerate(idxs):
        hit = lane == idx
        p_k = jnp.sum(jnp.where(hit, before, 0.0), axis=1, keepdims=True).astype(I32)
        w_k = jnp.sum(jnp.where(hit, w_sel, 0.0), axis=1, keepdims=True)
        eid = jnp.where(lane_o == kk, idx, eid)
        pos = jnp.where(lane_o == kk, p_k, pos)
        wgt = jnp.where(lane_o == kk, w_k, wgt)
    eid_ref[...] = eid
    pos_ref[...] = pos
    wgt_ref[...] = wgt


def router(rows, h, modv, layer, router_w, router_bias):
    t, d = h.shape
    ne = router_w.shape[1]
    tm = rows.tm // 2
    row = pl.BlockSpec((tm, LANES), lambda i: (i, 0))
    return pl.pallas_call(
        functools.partial(_router_kernel, tm, ne),
        grid=(t // tm,),
        in_specs=[rows.row_spec(tm), rows.mod_spec(layer, 4, tm), rows.mod_spec(layer, 3, tm),
                  pl.BlockSpec((d, ne), lambda i: (0, 0)), pl.BlockSpec((1, ne), lambda i: (0, 0))],
        out_specs=[row, row, row, pl.BlockSpec((8, ne), lambda i: (0, 0))],
        out_shape=[jax.ShapeDtypeStruct((t, LANES), I32), jax.ShapeDtypeStruct((t, LANES), I32),
                   jax.ShapeDtypeStruct((t, LANES), F32), jax.ShapeDtypeStruct((8, ne), F32)],
        scratch_shapes=[pltpu.VMEM((8, ne), F32)],
        compiler_params=_cparams(("arbitrary",)),
        name="router",
    )(h, modv, modv, router_w, router_bias.reshape(1, ne))


def _dispatch_kernel(tm, ne, n_all_tiles, pstart_ref, npad_ref, nt_ref, dest_ref, x_ref, xs_ref, zero_ref, sem, sem_pad):
    def pad_copy(e, r):
        return pltpu.make_async_copy(zero_ref.at[pl.ds(0, 1), :], xs_ref.at[pl.ds(pstart_ref[e] + r, 1), :], sem_pad)

    def tail_copy(j):
        return pltpu.make_async_copy(zero_ref, xs_ref.at[pl.ds(pl.multiple_of(j * TME, TME), TME), :], sem_pad)

    @pl.when(pl.program_id(0) == 0)
    def _():
        zero_ref[...] = jnp.zeros_like(zero_ref)

        def fill(e, c):
            lax.fori_loop(0, npad_ref[e], lambda r, c2: (pad_copy(e, r).start(), c2)[1], 0)
            return c

        lax.fori_loop(0, ne, fill, 0)
        lax.fori_loop(nt_ref[0], n_all_tiles, lambda j, c: (tail_copy(j).start(), c)[1], 0)

        def drain(e, c):
            lax.fori_loop(0, npad_ref[e], lambda r, c2: (pad_copy(e, r).wait(), c2)[1], 0)
            return c

        lax.fori_loop(0, ne, drain, 0)
        lax.fori_loop(nt_ref[0], n_all_tiles, lambda j, c: (tail_copy(j).wait(), c)[1], 0)

    def issue(t, c):
        for kk in range(TOP_K):
            pltpu.make_async_copy(x_ref.at[pl.ds(t, 1), :], xs_ref.at[pl.ds(dest_ref[0, t * TOP_K + kk], 1), :], sem).start()
        return c

    lax.fori_loop(0, tm, issue, 0)
    for _ in range(TOP_K):
        pltpu.make_async_copy(x_ref, x_ref, sem).wait()


def dispatch(vp, dest, pad_start, n_pad, n_tiles, n_rows):
    t, dh = vp.shape
    ne = pad_start.shape[0]
    tm = _pick_tile(t, (256, 128))
    gs = pltpu.PrefetchScalarGridSpec(
        num_scalar_prefetch=3,
        grid=(t // tm,),
        in_specs=[pl.BlockSpec((None, 1, tm * TOP_K), lambda i, ps, npd, nt: (i, 0, 0), memory_space=pltpu.SMEM),
                  pl.BlockSpec((tm, dh), lambda i, ps, npd, nt: (i, 0))],
        out_specs=pl.BlockSpec(memory_space=pl.ANY),
        scratch_shapes=[pltpu.VMEM((TME, dh), U32), pltpu.SemaphoreType.DMA, pltpu.SemaphoreType.DMA],
    )
    return pl.pallas_call(
        functools.partial(_dispatch_kernel, tm, ne, n_rows // TME),
        grid_spec=gs,
        out_shape=jax.ShapeDtypeStruct((n_rows, dh), U32),
        compiler_params=_cparams(("arbitrary",)),
        name="moe_dispatch",
    )(pad_start, n_pad, n_tiles, dest.reshape(t // tm, 1, tm * TOP_K), vp)


def _swiglu_kernel(eh, pack_out, te_ref, nt_ref, x_ref, wi_ref, wo_ref, o_ref, wib_ref, wob_ref):
    j = pl.program_id(0)

    @pl.when(j < nt_ref[0])
    def _():
        @pl.when((j == 0) | (te_ref[j] != te_ref[jnp.maximum(j - 1, 0)]))
        def _():
            wib_ref[...] = wi_ref[...].astype(wib_ref.dtype)
            wob_ref[...] = wo_ref[...].astype(wob_ref.dtype)

        ab = _bdot(_unpack_pairs(x_ref[...]), wib_ref[...])
        act = _silu(ab[:, :eh]) * ab[:, eh:]
        y = _bdot(act, wob_ref[...])
        o_ref[...] = _pack_pairs(y) if pack_out else y

    @pl.when(j >= nt_ref[0])
    def _():
        o_ref[...] = jnp.zeros_like(o_ref)


def grouped_swiglu(xs, w_in, w_out, tile_expert, n_tiles, pack_out):
    r, dh = xs.shape
    ne, d, eh2 = w_in.shape
    eh = eh2 // 2
    grid_n = r // TME
    out_w, out_dtype = (dh, U32) if pack_out else (d, F32)

    def row_map(j, te, nt):
        return (jnp.minimum(j, nt[0] - 1), 0)

    gs = pltpu.PrefetchScalarGridSpec(
        num_scalar_prefetch=2,
        grid=(grid_n,),
        in_specs=[pl.BlockSpec((TME, dh), row_map),
                  pl.BlockSpec((None, d, eh2), lambda j, te, nt: (te[j], 0, 0)),
                  pl.BlockSpec((None, eh, d), lambda j, te, nt: (te[j], 0, 0))],
        out_specs=pl.BlockSpec((TME, out_w), lambda j, te, nt: (j, 0)),
        scratch_shapes=[pltpu.VMEM((d, eh2), BF16), pltpu.VMEM((eh, d), BF16)],
    )
    return pl.pallas_call(
        functools.partial(_swiglu_kernel, eh, pack_out),
        grid_spec=gs,
        out_shape=jax.ShapeDtypeStruct((r, out_w), out_dtype),
        compiler_params=_cparams(("arbitrary",)),
        name="grouped_swiglu",
    )(tile_expert, n_tiles, xs, w_in, w_out)


def _combine_kernel(tm, alpha, has_next, dest_ref, destn_ref, wgt_ref, sh_ref, h_ref, gate_ref, lng_ref, lnb_ref,
                    sc_ref, sh2_ref, ys_ref, *rest):
    if has_next:
        hn_ref, u_ref, buf_ref, sem = rest
    else:
        hn_ref, buf_ref, sem = rest
    i = pl.program_id(0)
    slot = i % 2

    def gather(d_ref, s):
        def body(t, c):
            for kk in range(TOP_K):
                pltpu.make_async_copy(ys_ref.at[pl.ds(d_ref[0, t * TOP_K + kk], 1), :],
                                      buf_ref.at[s, kk, pl.ds(t, 1), :], sem.at[s]).start()
            return c
        lax.fori_loop(0, tm, body, 0)

    @pl.when(i == 0)
    def _():
        gather(dest_ref, 0)

    @pl.when(i + 1 < pl.num_programs(0))
    def _():
        gather(destn_ref, 1 - slot)

    pltpu.make_async_copy(buf_ref.at[slot], buf_ref.at[slot], sem.at[slot]).wait()

    wgt = wgt_ref[...]
    f = sh_ref[...]
    for kk in range(TOP_K):
        f = f + _lane_col(wgt, kk) * _unpack_pairs(buf_ref[slot, kk])
    y = _layer_norm_rows(alpha * h_ref[...] + gate_ref[...] * f, lng_ref[...], lnb_ref[...])
    hn_ref[...] = y
    if has_next:
        u_ref[...] = (y * (1.0 + sc_ref[...]) + sh2_ref[...]).astype(u_ref.dtype)


def moe_combine_ln(rows, ys, dest, wgt, shared, h, modv, layer, lng, lnb, alpha, has_next):
    tm = 128
    t, d = rows.t, rows.d
    dh = ys.shape[1]
    n_all = t // tm
    off = 0 if has_next else rows.n_ctx // tm
    n_out = n_all - off
    vec = pl.BlockSpec((1, d), lambda i: (0, 0))
    nl = layer + 1 if has_next else layer

    def row(width):
        return pl.BlockSpec((tm, width), lambda i: (i + off, 0))

    def mod(lyr, which):
        return pl.BlockSpec((None, 1, d), lambda i: ((lyr * MOD_ROWS + rows.seg(i + off, tm)) * 6 + which, 0, 0))

    out_specs = [pl.BlockSpec((tm, d), lambda i: (i, 0))]
    out_shape = [jax.ShapeDtypeStruct((n_out * tm, d), F32)]
    if has_next:
        out_specs.append(pl.BlockSpec((tm, d), lambda i: (i, 0)))
        out_shape.append(jax.ShapeDtypeStruct((n_out * tm, d), BF16))
    dest3 = dest.reshape(n_all, 1, tm * TOP_K)
    return pl.pallas_call(
        functools.partial(_combine_kernel, tm, alpha, has_next),
        grid=(n_out,),
        in_specs=[pl.BlockSpec((None, 1, tm * TOP_K), lambda i: (i + off, 0, 0), memory_space=pltpu.SMEM),
                  pl.BlockSpec((None, 1, tm * TOP_K), lambda i: (jnp.minimum(i + off + 1, n_all - 1), 0, 0),
                               memory_space=pltpu.SMEM),
                  row(LANES), row(d), row(d), mod(layer, 5), vec, vec, mod(nl, 1), mod(nl, 0),
                  pl.BlockSpec(memory_space=pl.ANY)],
        out_specs=out_specs,
        out_shape=out_shape,
        scratch_shapes=[pltpu.VMEM((2, TOP_K, tm, dh), U32), pltpu.SemaphoreType.DMA((2,))],
        compiler_params=_cparams(("arbitrary",)),
        name="moe_combine_ln",
    )(dest3, dest3, wgt, shared, h, modv, lng.reshape(1, d), lnb.reshape(1, d), modv, modv, ys)


def moe_ffn(rows, vp, h, modv, layer, lng, lnb, alpha, has_next, router_w, router_bias, exp_w_in, exp_w_out,
            sh_w_in, sh_w_out):
    t = vp.shape[0]
    ne = router_w.shape[1]
    eid, pos, wgt, cnt = router(rows, h, modv, layer, router_w, router_bias)
    counts = cnt[0].astype(I32)
    padded = (counts + TME - 1) // TME * TME
    ends = jnp.cumsum(padded)
    offs = ends - padded
    dest = offs[eid[:, :TOP_K]] + pos[:, :TOP_K]
    n_rows = (t * TOP_K // TME + ne) * TME
    tile_start = jnp.arange(n_rows // TME, dtype=I32) * TME
    tile_expert = jnp.minimum(jnp.sum((tile_start[:, None] >= ends[None, :]).astype(I32), axis=1), ne - 1)
    n_tiles = (ends[-1] // TME).reshape(1)

    xs = dispatch(vp, dest, offs + counts, padded - counts, n_tiles, n_rows)
    ys = grouped_swiglu(xs, exp_w_in, exp_w_out, tile_expert, n_tiles, True)
    shared = grouped_swiglu(vp, sh_w_in[None], sh_w_out[None], jnp.zeros((t // TME,), I32),
                            jnp.full((1,), t // TME, I32), False)
    return moe_combine_ln(rows, ys, dest, wgt, shared, h, modv, layer, lng, lnb, alpha, has_next)


def _pad_cols(w, n):
    return jnp.pad(w, ((0, 0), (0, n - w.shape[1])))


def _pad_row(v, n=LANES):
    v = v.reshape(1, -1).astype(F32)
    return jnp.pad(v, ((0, 0), (0, n - v.shape[1])))


def retention_mlstm_mixer(rows, order, u, w_in, w_out, ret_decay_rate, ret_norm_g, ml_gate_b, ml_norm_g):
    nh_r = ret_decay_rate.shape[-1]
    nh_m = ml_gate_b.shape[-1]
    ret_w = ret_norm_g.shape[0]
    ml_w = ml_norm_g.shape[0]
    hd_r, hd_m = ret_w // nh_r, ml_w // nh_m
    assert hd_r == hd_m
    main_w = 4 * ret_w + 4 * ml_w
    wb = w_in.astype(BF16)
    p = matmul(u, wb[:, :main_w], name="ab_in_proj")
    pg = matmul(u, _pad_cols(wb[:, main_w:], LANES), name="ab_gate_proj")

    pos = jnp.arange(rows.ctx + rows.seq, dtype=F32)
    half = hd_r // 2
    inv_freq = ROPE_BASE ** (-jnp.arange(half, dtype=F32) / half)
    ang = pos[:, None] * inv_freq[None, :]
    rate = jnp.broadcast_to(ret_decay_rate.astype(F32).reshape(2 * nh_r, 1, 1), (2 * nh_r, 1, LANES))
    y_r = retention(order, p, jnp.cos(ang), jnp.sin(ang), rate, ret_norm_g, nh_r, hd_r)

    gc, gr = ml_gates(pg, _pad_row(ml_gate_b), nh_m)
    y_m = mlstm(order, p, 4 * nh_r, gc, gr, ml_norm_g, nh_m, hd_m)
    return matmul2(y_r, y_m, w_out.astype(BF16), name="ab_out_proj")


def gated_deltanet_mixer(rows, order, u, w_in, conv_w, a_log, dt_bias, norm_g, w_out):
    nh = a_log.shape[-1]
    hd = norm_g.shape[0]
    dn_w = nh * hd
    wb = w_in.astype(BF16)
    p = matmul(u, wb[:, :4 * dn_w], name="dn_in_proj")
    pg = matmul(u, _pad_cols(wb[:, 4 * dn_w:], LANES), name="dn_gate_proj")
    conv_w9 = jnp.pad(conv_w.reshape(9, 3 * dn_w).astype(F32), ((0, 7), (0, 0)))
    qkv = dn_conv(rows, p, conv_w9, dn_w, hd)
    gc, gr, gt = dn_gates(pg, _pad_row(a_log), _pad_row(dt_bias), nh)
    y = gated_deltanet(order, qkv, p, gc, gr, gt, norm_g, nh, hd)
    return matmul(y, w_out.astype(BF16), name="dn_out_proj")


def kernel(x, c, ctx, c_ctx, ada_w, ada_b, ln_g, ln_b, ab_w_in, ab_w_out, ret_decay_rate, ret_norm_g, ml_gate_b, ml_norm_g, dn_w_in, dn_conv_w, dn_a_log, dn_dt_bias, dn_norm_g, dn_w_out, router_w, router_bias, exp_w_in, exp_w_out, sh_w_in, sh_w_out):
    b, seq, d = x.shape
    n_ctx = ctx.shape[1]
    depth = ada_w.shape[0]
    alpha = (2 * depth) ** 0.25
    rows = Rows(b, n_ctx, seq, d)
    order = ScanOrder(b, n_ctx, seq)
    assert b + 1 <= MOD_ROWS

    cvec = jnp.zeros((MOD_ROWS, d), F32).at[:b].set(c).at[b].set(c_ctx)
    modv = ada_table(cvec, ada_w, ada_b).reshape(depth * MOD_ROWS * 6, 1, d)

    h = jnp.concatenate([ctx.reshape(b * n_ctx, d), x.reshape(b * seq, d)], axis=0)
    u = modulate(rows, h, modv, 0, 1, 0)
    for layer in range(depth):
        j = layer // 2
        if layer % 2 == 0:
            z = retention_mlstm_mixer(rows, order, u, ab_w_in[j], ab_w_out[j], ret_decay_rate[j], ret_norm_g[j],
                                      ml_gate_b[j], ml_norm_g[j])
        else:
            z = gated_deltanet_mixer(rows, order, u, dn_w_in[j], dn_conv_w[j], dn_a_log[j], dn_dt_bias[j],
                                     dn_norm_g[j], dn_w_out[j])
        h, vp = ln_residual_pack(rows, h, z, modv, layer, ln_g[layer, 0], ln_b[layer, 0], alpha)
        has_next = layer < depth - 1
        outs = moe_ffn(rows, vp, h, modv, layer, ln_g[layer, 1], ln_b[layer, 1], alpha, has_next, router_w[layer],
                       router_bias[layer], exp_w_in[layer], exp_w_out[layer], sh_w_in[layer], sh_w_out[layer])
        if has_next:
            h, u = outs
        else:
            h = outs[0]
    return h.reshape(b, seq, d)
```

```python
import functools
import math

import jax
import jax.numpy as jnp
from jax import lax
from jax.experimental import pallas as pl
from jax.experimental.pallas import tpu as pltpu

F32 = jnp.float32
BF16 = jnp.bfloat16
I32 = jnp.int32
U32 = jnp.uint32

V7X_VMEM_BYTES = 64 * 1024 * 1024
VMEM_LIMIT = V7X_VMEM_BYTES - 8 * 1024 * 1024
LANES = 128

GRID_W = 64
CHUNK = 64
TOP_K = 8
N_GROUPS = 8
TOPK_GROUPS = 4
ROUTED_SCALE = 2.5
ROPE_BASE = 10000.0
LN_EPS = 1e-5

LB = 256
SCAN_HG = 2
TME = 512
MOD_ROWS = 8


def _cparams(sem):
    return pltpu.CompilerParams(dimension_semantics=sem, vmem_limit_bytes=VMEM_LIMIT)


def _bdot(a, b):
    return jnp.dot(a.astype(BF16), b.astype(BF16), preferred_element_type=F32)


def _bdot_nt(a, b):
    return lax.dot_general(a.astype(BF16), b.astype(BF16), (((1,), (1,)), ((), ())), preferred_element_type=F32)


def _bdot_tn(a, b):
    return lax.dot_general(a.astype(BF16), b.astype(BF16), (((0,), (0,)), ((), ())), preferred_element_type=F32)


def _fdot(a, b):
    return jnp.dot(a, b, preferred_element_type=F32, precision=lax.Precision.HIGHEST)


def _sigmoid(x):
    return 1.0 / (1.0 + jnp.exp(-x))


def _silu(x):
    return x * _sigmoid(x)


def _softplus(x):
    return jnp.maximum(x, 0.0) + jnp.log(1.0 + jnp.exp(-jnp.abs(x)))


def _log_sigmoid(x):
    return -_softplus(-x)


def _lane_col(x, lane):
    idx = lax.broadcasted_iota(I32, x.shape, 1)
    return jnp.sum(jnp.where(idx == lane, x, 0.0), axis=1, keepdims=True)


def _ada_kernel(c_ref, w_ref, b_ref, o_ref):
    o_ref[...] = _bdot(_silu(c_ref[...]), w_ref[...]) + b_ref[...]


def ada_table(cvec, ada_w, ada_b):
    depth, d, n = ada_w.shape
    tn = 512
    return pl.pallas_call(
        _ada_kernel,
        grid=(depth, n // tn),
        in_specs=[pl.BlockSpec((MOD_ROWS, d), lambda l, j: (0, 0)),
                  pl.BlockSpec((None, d, tn), lambda l, j: (l, 0, j)),
                  pl.BlockSpec((None, 1, tn), lambda l, j: (l, 0, j))],
        out_specs=pl.BlockSpec((None, MOD_ROWS, tn), lambda l, j: (l, 0, j)),
        out_shape=jax.ShapeDtypeStruct((depth, MOD_ROWS, n), F32),
        compiler_params=_cparams(("parallel", "parallel")),
        name="ada_table",
    )(cvec, ada_w, ada_b.reshape(depth, 1, n))


class Rows:
    def __init__(self, b, ctx, seq, d):
        self.b, self.ctx, self.seq, self.d = b, ctx, seq, d
        self.n_ctx = b * ctx
        self.t = b * ctx + b * seq
        self.tm = math.gcd(math.gcd(self.n_ctx, seq), 512)

    def seg(self, i, tm):
        n_ctx_tiles = self.n_ctx // tm
        return jnp.where(i < n_ctx_tiles, self.b, (i - n_ctx_tiles) // (self.seq // tm))

    def mod_spec(self, layer, which, tm):
        return pl.BlockSpec((None, 1, self.d), lambda i: ((layer * MOD_ROWS + self.seg(i, tm)) * 6 + which, 0, 0))

    def row_spec(self, tm, width=None):
        return pl.BlockSpec((tm, self.d if width is None else width), lambda i: (i, 0))


def _mod_kernel(h_ref, sc_ref, sh_ref, o_ref):
    o_ref[...] = (h_ref[...] * (1.0 + sc_ref[...]) + sh_ref[...]).astype(o_ref.dtype)


def modulate(rows, h, modv, layer, sc_idx, sh_idx):
    tm = rows.tm
    return pl.pallas_call(
        _mod_kernel,
        grid=(rows.t // tm,),
        in_specs=[rows.row_spec(tm), rows.mod_spec(layer, sc_idx, tm), rows.mod_spec(layer, sh_idx, tm)],
        out_specs=rows.row_spec(tm),
        out_shape=jax.ShapeDtypeStruct((rows.t, rows.d), BF16),
        compiler_params=_cparams(("parallel",)),
        name="modulate",
    )(h, modv, modv)


def _layer_norm_rows(r, g, b):
    mu = jnp.mean(r, axis=-1, keepdims=True)
    xc = r - mu
    var = jnp.mean(xc * xc, axis=-1, keepdims=True)
    return xc * lax.rsqrt(var + LN_EPS) * g + b


def _mm_kernel(x_ref, w_ref, o_ref):
    o_ref[...] = _bdot(x_ref[...], w_ref[...]).astype(o_ref.dtype)


def _pick_tile(n, prefs):
    for p in prefs:
        if n % p == 0:
            return p
    return n


def matmul(x, w, out_dtype=F32, name="matmul"):
    m, k = x.shape
    n = w.shape[1]
    tm = _pick_tile(m, (768, 512, 256, 128))
    tn = _pick_tile(n, (1024, 512, 256, 128))
    return pl.pallas_call(
        _mm_kernel,
        grid=(n // tn, m // tm),
        in_specs=[pl.BlockSpec((tm, k), lambda j, i: (i, 0)), pl.BlockSpec((k, tn), lambda j, i: (0, j))],
        out_specs=pl.BlockSpec((tm, tn), lambda j, i: (i, j)),
        out_shape=jax.ShapeDtypeStruct((m, n), out_dtype),
        compiler_params=_cparams(("parallel", "parallel")),
        name=name,
    )(x, w)


def _mm2_kernel(x1_ref, x2_ref, w1_ref, w2_ref, o_ref):
    o_ref[...] = _bdot(x1_ref[...], w1_ref[...]) + _bdot(x2_ref[...], w2_ref[...])


def matmul2(x1, x2, w, name="matmul2"):
    m, k1 = x1.shape
    k2 = x2.shape[1]
    n = w.shape[1]
    tm = _pick_tile(m, (768, 512, 256, 128))
    tn = _pick_tile(n, (1024, 512, 256, 128))
    return pl.pallas_call(
        _mm2_kernel,
        grid=(n // tn, m // tm),
        in_specs=[pl.BlockSpec((tm, k1), lambda j, i: (i, 0)), pl.BlockSpec((tm, k2), lambda j, i: (i, 0)),
                  pl.BlockSpec((k1, tn), lambda j, i: (0, j)), pl.BlockSpec((k2, tn), lambda j, i: (k1 // k2, j))],
        out_specs=pl.BlockSpec((tm, tn), lambda j, i: (i, j)),
        out_shape=jax.ShapeDtypeStruct((m, n), F32),
        compiler_params=_cparams(("parallel", "parallel")),
        name=name,
    )(x1, x2, w, w)


class ScanOrder:
    def __init__(self, b, ctx, seq):
        assert ctx % LB == 0 and seq % LB == 0
        self.b = b
        self.nc = ctx // LB
        self.nl = seq // LB
        self.steps = self.nc + self.nl

    def row_block(self, direction, bi, s):
        nc, nl = self.nc, self.nl
        if direction == 0:
            c, l = s, s - nc
        else:
            c, l = nc - 1 - s, nl - 1 - (s - nc)
        return jnp.where(s < nc, bi * nc + c, self.b * nc + bi * nl + l)

    def pos_block(self, direction, s):
        nc, nl = self.nc, self.nl
        if direction == 0:
            return s
        return jnp.where(s < nc, nc - 1 - s, nc + nl - 1 - (s - nc))


def _causal(direction, n):
    ri = lax.broadcasted_iota(I32, (n, n), 0)
    ci = lax.broadcasted_iota(I32, (n, n), 1)
    rel = ri - ci if direction == 0 else ci - ri
    return rel >= 0, rel > 0, rel


def _rope(t, cos, sin):
    half = t.shape[-1] // 2
    t1, t2 = t[:, :half], t[:, half:]
    return jnp.concatenate([t1 * cos - t2 * sin, t1 * sin + t2 * cos], axis=-1)


def _head_ln_gate(o, g, gate_act):
    mu = jnp.mean(o, axis=-1, keepdims=True)
    xc = o - mu
    var = jnp.mean(xc * xc, axis=-1, keepdims=True)
    return xc * lax.rsqrt(var + LN_EPS) * g * gate_act


def _ret_kernel(direction, hd, *refs):
    ng = SCAN_HG
    s_refs = refs[-ng:]
    if direction == 0:
        q_ref, k_ref, v_ref, cos_ref, sin_ref, rate_ref, o_ref = refs[:-ng]
    else:
        q_ref, k_ref, v_ref, cos_ref, sin_ref, rate_ref, of_ref, g_ref, ng_ref, o_ref = refs[:-ng]
    step = pl.program_id(2)

    @pl.when(step == 0)
    def _():
        for s_ref in s_refs:
            s_ref[...] = jnp.zeros_like(s_ref)

    incl, _, rel = _causal(direction, LB)
    relf = jnp.maximum(rel, 0).astype(F32)
    ri = lax.broadcasted_iota(I32, (LB, 1), 0).astype(F32)
    q_pow, k_pow = (ri + 1.0, LB - 1.0 - ri) if direction == 0 else (LB - ri, ri)
    cos, sin = cos_ref[...], sin_ref[...]
    heads = range(ng)
    sls = [slice(hh * hd, (hh + 1) * hd) for hh in heads]
    lgs = [-jnp.exp(rate_ref[hh])[:, :1] for hh in heads]
    qs = [_rope(q_ref[:, sl], cos, sin) for sl in sls]
    ks = [_rope(k_ref[:, sl], cos, sin) * hd ** -0.5 for sl in sls]
    vs = [v_ref[:, sl] for sl in sls]
    ss = [s_ref[...] for s_ref in s_refs]
    atts = [_bdot_nt(q, k) * jnp.where(incl, jnp.exp(lg * relf), 0.0) for q, k, lg in zip(qs, ks, lgs)]
    inter = [_bdot(q * jnp.exp(lg * q_pow), s) for q, lg, s in zip(qs, lgs, ss)]
    outs = [_bdot(att, v) + oi for att, v, oi in zip(atts, vs, inter)]
    for hh in heads:
        lg = lgs[hh]
        s_refs[hh][...] = ss[hh] * jnp.exp(lg * LB) + _bdot_tn(ks[hh] * jnp.exp(lg * k_pow), vs[hh])
    for hh in heads:
        sl = sls[hh]
        if direction == 0:
            o_ref[:, sl] = outs[hh]
        else:
            o_ref[:, sl] = _head_ln_gate(outs[hh] + of_ref[:, sl], ng_ref[:, sl], _silu(g_ref[:, sl])).astype(o_ref.dtype)


def retention(order, p, cos, sin, rate, norm_g, nh, hd):
    t = p.shape[0]
    b = order.b
    half = hd // 2
    g = SCAN_HG
    assert nh % g == 0
    gw = g * hd

    def run(direction, extra_in, extra_specs, out_dtype):
        def blk(col0):
            return pl.BlockSpec((LB, gw), lambda bi, h, s: (order.row_block(direction, bi, s), col0 // g + h))
        in_specs = [blk(0), blk(nh), blk(2 * nh),
                    pl.BlockSpec((LB, half), lambda bi, h, s: (order.pos_block(direction, s), 0)),
                    pl.BlockSpec((LB, half), lambda bi, h, s: (order.pos_block(direction, s), 0)),
                    pl.BlockSpec((g, 1, LANES), lambda bi, h, s: (direction * nh // g + h, 0, 0))] + extra_specs
        return pl.pallas_call(
            functools.partial(_ret_kernel, direction, hd),
            grid=(b, nh // g, order.steps),
            in_specs=in_specs,
            out_specs=pl.BlockSpec((LB, gw), lambda bi, h, s: (order.row_block(direction, bi, s), h)),
            out_shape=jax.ShapeDtypeStruct((t, nh * hd), out_dtype),
            scratch_shapes=[pltpu.VMEM((hd, hd), F32)] * g,
            compiler_params=_cparams(("parallel", "parallel", "arbitrary")),
            name=f"retention_d{direction}",
        )(p, p, p, cos, sin, rate, *extra_in)

    o_fwd = run(0, [], [], F32)
    return run(1, [o_fwd, p, norm_g.reshape(1, nh * hd)],
               [pl.BlockSpec((LB, gw), lambda bi, h, s: (order.row_block(1, bi, s), h)),
                pl.BlockSpec((LB, gw), lambda bi, h, s: (order.row_block(1, bi, s), 3 * nh // g + h)),
                pl.BlockSpec((1, gw), lambda bi, h, s: (0, h))], BF16)


def _ml_gate_kernel(nh, g_ref, b_ref, gc_ref, gr_ref):
    g = g_ref[...] + b_ref[...]
    ls = _log_sigmoid(g)
    incl_f, _, _ = _causal(0, LB)
    incl_b, _, _ = _causal(1, LB)
    pre = _fdot(incl_f.astype(F32), ls)
    suf = _fdot(incl_b.astype(F32), ls)
    lane = lax.broadcasted_iota(I32, g.shape, 1)
    is_f = (lane // nh) % 2 == 1
    is_d1 = lane // (2 * nh) == 1
    out = jnp.where(is_f, jnp.where(is_d1, suf, pre), g)
    gc_ref[...] = out
    gr_ref[...] = out.T


def ml_gates(pg, bias_row, nh):
    t = pg.shape[0]
    return pl.pallas_call(
        functools.partial(_ml_gate_kernel, nh),
        grid=(t // LB,),
        in_specs=[pl.BlockSpec((LB, LANES), lambda i: (i, 0)), pl.BlockSpec((1, LANES), lambda i: (0, 0))],
        out_specs=[pl.BlockSpec((LB, LANES), lambda i: (i, 0)), pl.BlockSpec((None, LANES, LB), lambda i: (i, 0, 0))],
        out_shape=[jax.ShapeDtypeStruct((t, LANES), F32), jax.ShapeDtypeStruct((t // LB, LANES, LB), F32)],
        compiler_params=_cparams(("parallel",)),
        name="ml_gates",
    )(pg, bias_row)


def _ml_kernel(direction, nh, hd, *refs):
    ng = SCAN_HG
    c_refs, n_refs, m_refs = refs[-3 * ng:-2 * ng], refs[-2 * ng:-ng], refs[-ng:]
    if direction == 0:
        q_ref, k_ref, v_ref, gc_ref, gr_ref, o_ref = refs[:-3 * ng]
    else:
        q_ref, k_ref, v_ref, gc_ref, gr_ref, of_ref, g_ref, ng_ref, o_ref = refs[:-3 * ng]
    grp = pl.program_id(1)
    step = pl.program_id(2)

    @pl.when(step == 0)
    def _():
        for ref in refs[-3 * ng:]:
            ref[...] = jnp.zeros_like(ref)

    gc = gc_ref[...]
    incl, _, _ = _causal(direction, LB)
    last = LB - 1 if direction == 0 else 0
    heads = range(ng)
    sls = [slice(hh * hd, (hh + 1) * hd) for hh in heads]
    lanes_i = [direction * 2 * nh + grp * ng + hh for hh in heads]
    ig_cols = [_lane_col(gc, li) for li in lanes_i]
    b_cols = [_lane_col(gc, li + nh) for li in lanes_i]
    d_logs = [jnp.where(incl, b_col - gr_ref[pl.ds(li + nh, 1), :] + gr_ref[pl.ds(li, 1), :], -jnp.inf)
              for b_col, li in zip(b_cols, lanes_i)]
    inter_logs = [b_col + m_ref[0:1, 0:1] for b_col, m_ref in zip(b_cols, m_refs)]
    m_rows = [jnp.maximum(il, jnp.max(dl, axis=1, keepdims=True)) for il, dl in zip(inter_logs, d_logs)]
    inter_ws = [jnp.exp(il - mr) for il, mr in zip(inter_logs, m_rows)]
    qs = [q_ref[:, sl] for sl in sls]
    ks = [k_ref[:, sl] * hd ** -0.5 for sl in sls]
    vs = [v_ref[:, sl] for sl in sls]
    cs = [c_ref[...] for c_ref in c_refs]
    ns = [n_ref[0:1, :] for n_ref in n_refs]
    ss = [_bdot_nt(q, k) * jnp.exp(dl - mr) for q, k, dl, mr in zip(qs, ks, d_logs, m_rows)]
    qcs = [_bdot(q, c) for q, c in zip(qs, cs)]
    nums = [_bdot(s, v) + iw * qc for s, v, iw, qc in zip(ss, vs, inter_ws, qcs)]
    houts = []
    for hh in heads:
        den = (jnp.sum(ss[hh], axis=1, keepdims=True)
               + inter_ws[hh] * jnp.sum(qs[hh] * ns[hh], axis=1, keepdims=True))
        houts.append(nums[hh] / jnp.maximum(jnp.abs(den), jnp.exp(-m_rows[hh])))
    for hh in heads:
        b_col, m_row = b_cols[hh], m_rows[hh]
        m_last = m_row[last:last + 1, :]
        dec_last = inter_ws[hh][last:last + 1, :]
        w_last = jnp.exp(b_col[last:last + 1, :] - b_col + ig_cols[hh] - m_last)
        kw = ks[hh] * w_last
        c_refs[hh][...] = dec_last * cs[hh] + _bdot_tn(kw, vs[hh])
        n_refs[hh][0:1, :] = dec_last * ns[hh] + jnp.sum(kw, axis=0, keepdims=True)
        m_refs[hh][...] = jnp.broadcast_to(m_last, m_refs[hh].shape)
    for hh in heads:
        sl = sls[hh]
        if direction == 0:
            o_ref[:, sl] = houts[hh]
        else:
            o_ref[:, sl] = _head_ln_gate(houts[hh] + of_ref[:, sl], ng_ref[:, sl],
                                         _sigmoid(g_ref[:, sl])).astype(o_ref.dtype)


def mlstm(order, p, col0, gc, gr, norm_g, nh, hd):
    t = p.shape[0]
    b = order.b
    g = SCAN_HG
    assert nh % g == 0 and col0 % g == 0
    gw = g * hd

    def run(direction, extra_in, extra_specs, out_dtype):
        def blk(c):
            return pl.BlockSpec((LB, gw), lambda bi, h, s: (order.row_block(direction, bi, s), (col0 + c) // g + h))
        in_specs = [blk(0), blk(nh), blk(2 * nh),
                    pl.BlockSpec((LB, LANES), lambda bi, h, s: (order.row_block(direction, bi, s), 0)),
                    pl.BlockSpec((None, LANES, LB), lambda bi, h, s: (order.row_block(direction, bi, s), 0, 0))]
        return pl.pallas_call(
            functools.partial(_ml_kernel, direction, nh, hd),
            grid=(b, nh // g, order.steps),
            in_specs=in_specs + extra_specs,
            out_specs=pl.BlockSpec((LB, gw), lambda bi, h, s: (order.row_block(direction, bi, s), h)),
            out_shape=jax.ShapeDtypeStruct((t, nh * hd), out_dtype),
            scratch_shapes=([pltpu.VMEM((hd, hd), F32)] * g + [pltpu.VMEM((8, hd), F32)] * g
                            + [pltpu.VMEM((8, LANES), F32)] * g),
            compiler_params=_cparams(("parallel", "parallel", "arbitrary")),
            name=f"mlstm_d{direction}",
        )(p, p, p, gc, gr, *extra_in)

    o_fwd = run(0, [], [], F32)
    return run(1, [o_fwd, p, norm_g.reshape(1, nh * hd)],
               [pl.BlockSpec((LB, gw), lambda bi, h, s: (order.row_block(1, bi, s), h)),
                pl.BlockSpec((LB, gw), lambda bi, h, s: (order.row_block(1, bi, s), (col0 + 3 * nh) // g + h)),
                pl.BlockSpec((1, gw), lambda bi, h, s: (0, h))], BF16)


CONV_HALO = 128


def _conv_kernel(tb, tc, n_ctx, ctx, seq, n_qk_tiles, n_q_tiles, hd, xp_ref, xc_ref, xn_ref, w_ref, o_ref, buf_ref, lr_ref):
    i = pl.program_id(0)
    j = pl.program_id(1)
    is_ctx = i * tb < n_ctx
    lat0 = (i * tb - n_ctx) % seq
    top = jnp.logical_not(is_ctx) & (lat0 == 0)
    bottom = jnp.logical_not(is_ctx) & (lat0 == seq - tb)
    buf_ref[0:CONV_HALO, :] = jnp.where(top, 0.0, xp_ref[...])
    buf_ref[CONV_HALO:CONV_HALO + tb, :] = xc_ref[...]
    buf_ref[CONV_HALO + tb:, :] = jnp.where(bottom, 0.0, xn_ref[...])

    ext = tb + 2 * GRID_W
    base = CONV_HALO - GRID_W
    r = lax.broadcasted_iota(I32, (ext, 1), 0)
    col = jnp.where(is_ctx, (r + (ctx - GRID_W % ctx)) % ctx, r % GRID_W)
    n_c = jnp.where(is_ctx, ctx, GRID_W)
    lr_ref[0] = jnp.where(col >= 1, buf_ref[base - 1:base - 1 + ext, :], 0.0)
    lr_ref[1] = jnp.where(col + 1 < n_c, buf_ref[base + 1:base + 1 + ext, :], 0.0)

    acc = None
    for dr in range(3):
        wl, wc, wr = (w_ref[dr * 3 + dc:dr * 3 + dc + 1, :] for dc in range(3))
        if dr != 1:
            wl, wc, wr = (jnp.where(is_ctx, 0.0, w) for w in (wl, wc, wr))
        o0 = CONV_HALO + (dr - 1) * GRID_W
        e0 = dr * GRID_W
        term = (lr_ref[0, e0:e0 + tb, :] * wl + buf_ref[o0:o0 + tb, :] * wc + lr_ref[1, e0:e0 + tb, :] * wr)
        acc = term if acc is None else acc + term
    y = _silu(acc)
    qscale = jnp.where(j < n_q_tiles, hd ** -0.5, 1.0)
    is_qk = j < n_qk_tiles
    for g in range(tc // hd):
        yg = y[:, g * hd:(g + 1) * hd]
        nrm = lax.rsqrt(jnp.sum(yg * yg, axis=1, keepdims=True) + 1e-6) * qscale
        o_ref[:, g * hd:(g + 1) * hd] = yg * jnp.where(is_qk, nrm, 1.0)


def dn_conv(rows, p, conv_w9, dn_w, hd):
    t = rows.t
    tb = rows.tm
    tc = 512
    hb = tb // CONV_HALO
    n_halo = t // CONV_HALO
    kern = functools.partial(_conv_kernel, tb, tc, rows.n_ctx, rows.ctx, rows.seq, 2 * dn_w // tc, dn_w // tc, hd)
    return pl.pallas_call(
        kern,
        grid=(t // tb, 3 * dn_w // tc),
        in_specs=[pl.BlockSpec((CONV_HALO, tc), lambda i, j: (jnp.maximum(i * hb - 1, 0), j)),
                  pl.BlockSpec((tb, tc), lambda i, j: (i, j)),
                  pl.BlockSpec((CONV_HALO, tc), lambda i, j: (jnp.minimum(i * hb + hb, n_halo - 1), j)),
                  pl.BlockSpec((16, tc), lambda i, j: (0, j))],
        out_specs=pl.BlockSpec((tb, tc), lambda i, j: (i, j)),
        out_shape=jax.ShapeDtypeStruct((t, 3 * dn_w), F32),
        scratch_shapes=[pltpu.VMEM((tb + 2 * CONV_HALO, tc), F32), pltpu.VMEM((2, tb + 2 * GRID_W, tc), F32)],
        compiler_params=_cparams(("parallel", "parallel")),
        name="dn_conv",
    )(p, p, p, conv_w9)


def _chunk_masks(direction):
    ri = lax.broadcasted_iota(I32, (LB, LB), 0)
    ci = lax.broadcasted_iota(I32, (LB, LB), 1)
    same = (ri // CHUNK) == (ci // CHUNK)
    rel = ri - ci if direction == 0 else ci - ri
    return same & (rel >= 0), same & (rel > 0), same


def _dn_gate_kernel(nh, ab_ref, alog_ref, dtb_ref, gc_ref, gr_ref, gt_ref):
    x = ab_ref[...]
    lane = lax.broadcasted_iota(I32, x.shape, 1)
    g = -jnp.exp(alog_ref[...]) * _softplus(x + dtb_ref[...])
    g = jnp.where(lane < 2 * nh, g, 0.0)
    incl_f, _, same = _chunk_masks(0)
    incl_b, _, _ = _chunk_masks(1)
    pre = _fdot(incl_f.astype(F32), g)
    suf = _fdot(incl_b.astype(F32), g)
    tot = _fdot(same.astype(F32), g)
    out = jnp.where(lane < nh, pre, jnp.where(lane < 2 * nh, suf, _sigmoid(x)))
    gc_ref[...] = out
    gr_ref[...] = out.T
    gt_ref[...] = tot


def dn_gates(pg, alog_row, dtb_row, nh):
    t = pg.shape[0]
    return pl.pallas_call(
        functools.partial(_dn_gate_kernel, nh),
        grid=(t // LB,),
        in_specs=[pl.BlockSpec((LB, LANES), lambda i: (i, 0)), pl.BlockSpec((1, LANES), lambda i: (0, 0)),
                  pl.BlockSpec((1, LANES), lambda i: (0, 0))],
        out_specs=[pl.BlockSpec((LB, LANES), lambda i: (i, 0)), pl.BlockSpec((None, LANES, LB), lambda i: (i, 0, 0)),
                   pl.BlockSpec((LB, LANES), lambda i: (i, 0))],
        out_shape=[jax.ShapeDtypeStruct((t, LANES), F32), jax.ShapeDtypeStruct((t // LB, LANES, LB), F32),
                   jax.ShapeDtypeStruct((t, LANES), F32)],
        compiler_params=_cparams(("parallel",)),
        name="dn_gates",
    )(pg, alog_row, dtb_row)


DN_HG = 4
DN_SCAN_HG = 8


def _dn_ut_kernel(nh, hd, q_ref, k_ref, v_ref, gc_ref, gr_ref, gt_ref, *refs):
    outs = (refs[0:5], refs[5:10])
    a_ref, t_ref, x_ref, rhs_ref = refs[10:]
    hg = pl.program_id(1)
    ri = lax.broadcasted_iota(I32, (LB, LB), 0)
    ci = lax.broadcasted_iota(I32, (LB, LB), 1)
    eye = (ri == ci).astype(F32)
    same = (ri // CHUNK) == (ci // CHUNK)
    gcb = gc_ref[...]
    gtb = gt_ref[...]
    chains = [(d, hh) for d in (0, 1) for hh in range(DN_HG)]
    for c, (d, hh) in enumerate(chains):
        u_ref, w_ref, qg_ref, kd_ref, att_ref = outs[d]
        rel = ri - ci if d == 0 else ci - ri
        lane_g = d * nh + hg * DN_HG + hh
        gc_col = _lane_col(gcb, lane_g)
        beta = _lane_col(gcb, 2 * nh + lane_g)
        gt_col = _lane_col(gtb, lane_g)
        gc_row = gr_ref[pl.ds(lane_g, 1), :]
        sl = slice(hh * hd, (hh + 1) * hd)
        q, k, v = q_ref[:, sl], k_ref[:, sl], v_ref[:, sl]
        decay = jnp.exp(jnp.where(same & (rel >= 0), gc_col - gc_row, -jnp.inf))
        kb = k * beta
        a = jnp.where(rel > 0, _bdot_nt(kb, k) * decay, 0.0)
        a_ref[c] = a
        t_ref[c] = eye - jnp.where((ri // 2) == (ci // 2), a, 0.0)
        eg = jnp.exp(gc_col)
        rhs_ref[c] = jnp.concatenate([v * beta, kb * eg], axis=1).astype(rhs_ref.dtype)
        qg_ref[:, sl] = (q * eg).astype(qg_ref.dtype)
        kd_ref[sl, :] = (k * jnp.exp(gt_col - gc_col)).T.astype(kd_ref.dtype)
        att_ref[:, hh * LB:(hh + 1) * LB] = (_bdot_nt(q, k) * decay).astype(att_ref.dtype)
    s = 2
    while s < CHUNK:
        m = ((ri // (2 * s)) == (ci // (2 * s))) & ((ri // s) != (ci // s))
        for c in range(len(chains)):
            x_ref[c] = _bdot(t_ref[c], jnp.where(m, a_ref[c], 0.0)).astype(x_ref.dtype)
        for c in range(len(chains)):
            t = t_ref[c]
            t_ref[c] = t - _bdot(x_ref[c], t)
        s *= 2
    for c, (d, hh) in enumerate(chains):
        u_ref, w_ref = outs[d][0], outs[d][1]
        sl = slice(hh * hd, (hh + 1) * hd)
        sol = _bdot(t_ref[c], rhs_ref[c])
        u_ref[:, sl] = sol[:, :hd]
        w_ref[:, sl] = sol[:, hd:].astype(w_ref.dtype)


def _dn_scan_kernel(direction, nh, hd, hgs, *refs):
    s_refs, vn_refs = refs[-2 * hgs:-hgs], refs[-hgs:]
    if direction == 0:
        u_ref, w_ref, qg_ref, kd_ref, att_ref, gt_ref, o_ref = refs[:-2 * hgs]
    else:
        u_ref, w_ref, qg_ref, kd_ref, att_ref, gt_ref, of_ref, z_ref, ng_ref, o_ref = refs[:-2 * hgs]
    hg = pl.program_id(1)
    step = pl.program_id(2)

    @pl.when(step == 0)
    def _():
        for s_ref in s_refs:
            s_ref[...] = jnp.zeros_like(s_ref)

    for vn_ref in vn_refs:
        vn_ref[...] = jnp.zeros_like(vn_ref)
    gtb = gt_ref[...]
    egs = [jnp.exp(_lane_col(gtb, direction * nh + hg * hgs + hh)) for hh in range(hgs)]
    n_chunks = LB // CHUNK
    for ci in range(n_chunks):
        cc = ci if direction == 0 else n_chunks - 1 - ci
        rs = slice(cc * CHUNK, (cc + 1) * CHUNK)
        sls = [slice(hh * hd, (hh + 1) * hd) for hh in range(hgs)]
        ss = [s_ref[...] for s_ref in s_refs]
        rr = [_bdot(jnp.concatenate([w_ref[rs, sl], qg_ref[rs, sl]], axis=0), s)
              for sl, s in zip(sls, ss)]
        for hh in range(hgs):
            v_new = u_ref[rs, sls[hh]] - rr[hh][:CHUNK]
            vn_refs[hh][rs, :] = v_new
            parts = [jnp.zeros((n, hd), F32) for n in (cc * CHUNK,) if n] + [v_new]
            parts += [jnp.zeros((n, hd), F32) for n in (LB - (cc + 1) * CHUNK,) if n]
            s_refs[hh][...] = (egs[hh][cc * CHUNK:cc * CHUNK + 1, :] * ss[hh]
                               + _bdot(kd_ref[sls[hh], :], jnp.concatenate(parts, axis=0)))
        for hh in range(hgs):
            sl = sls[hh]
            o = rr[hh][CHUNK:] + _bdot(att_ref[rs, hh * LB:(hh + 1) * LB], vn_refs[hh][...])
            if direction == 0:
                o_ref[rs, sl] = o
            else:
                o = o + of_ref[rs, sl]
                rms = lax.rsqrt(jnp.mean(o * o, axis=1, keepdims=True) + 1e-6)
                o_ref[rs, sl] = (o * rms * ng_ref[...] * _silu(z_ref[rs, sl])).astype(o_ref.dtype)


def gated_deltanet(order, qkv, p, gc, gr, gt, norm_g, nh, hd):
    t = qkv.shape[0]
    b = order.b
    w = nh * hd
    gw = DN_HG * hd
    ng = nh // DN_HG
    nchain = 2 * DN_HG

    per_dir_specs = ([pl.BlockSpec((LB, gw), lambda i, g: (i, g))] * 3 + [pl.BlockSpec((gw, LB), lambda i, g: (g, i))]
                     + [pl.BlockSpec((LB, DN_HG * LB), lambda i, g: (i, g))])
    per_dir_shape = ([jax.ShapeDtypeStruct((t, w), F32)] + [jax.ShapeDtypeStruct((t, w), BF16)] * 2
                     + [jax.ShapeDtypeStruct((w, t), BF16), jax.ShapeDtypeStruct((t, nh * LB), BF16)])
    ut = pl.pallas_call(
        functools.partial(_dn_ut_kernel, nh, hd),
        grid=(t // LB, ng),
        in_specs=[pl.BlockSpec((LB, gw), lambda i, g: (i, g)), pl.BlockSpec((LB, gw), lambda i, g: (i, ng + g)),
                  pl.BlockSpec((LB, gw), lambda i, g: (i, 2 * ng + g)),
                  pl.BlockSpec((LB, LANES), lambda i, g: (i, 0)),
                  pl.BlockSpec((None, LANES, LB), lambda i, g: (i, 0, 0)),
                  pl.BlockSpec((LB, LANES), lambda i, g: (i, 0))],
        out_specs=per_dir_specs * 2,
        out_shape=per_dir_shape * 2,
        scratch_shapes=[pltpu.VMEM((nchain, LB, LB), F32), pltpu.VMEM((nchain, LB, LB), F32),
                        pltpu.VMEM((nchain, LB, LB), BF16), pltpu.VMEM((nchain, LB, 2 * hd), BF16)],
        compiler_params=_cparams(("parallel", "parallel")),
        name="dn_ut",
    )(qkv, qkv, qkv, gc, gr, gt)
    outs = (ut[0:5], ut[5:10])

    hgs = min(DN_SCAN_HG, nh)
    sw = hgs * hd
    nsg = nh // hgs

    def run(direction, extra_in, extra_specs, out_dtype):
        def blk(width):
            return pl.BlockSpec((LB, width), lambda bi, g, s: (order.row_block(direction, bi, s), g))
        return pl.pallas_call(
            functools.partial(_dn_scan_kernel, direction, nh, hd, hgs),
            grid=(b, nsg, order.steps),
            in_specs=[blk(sw)] * 3 + [pl.BlockSpec((sw, LB), lambda bi, g, s: (g, order.row_block(direction, bi, s))),
                      blk(hgs * LB),
                      pl.BlockSpec((LB, LANES), lambda bi, g, s: (order.row_block(direction, bi, s), 0))] + extra_specs,
            out_specs=blk(sw),
            out_shape=jax.ShapeDtypeStruct((t, w), out_dtype),
            scratch_shapes=[pltpu.VMEM((hd, hd), F32)] * hgs + [pltpu.VMEM((LB, hd), F32)] * hgs,
            compiler_params=_cparams(("parallel", "parallel", "arbitrary")),
            name=f"dn_scan_d{direction}",
        )(*outs[direction], gt, *extra_in)

    o_fwd = run(0, [], [], F32)
    return run(1, [o_fwd, p, norm_g.reshape(1, hd)],
               [pl.BlockSpec((LB, sw), lambda bi, g, s: (order.row_block(1, bi, s), g)),
                pl.BlockSpec((LB, sw), lambda bi, g, s: (order.row_block(1, bi, s), 3 * nsg + g)),
                pl.BlockSpec((1, hd), lambda bi, g, s: (0, 0))], BF16)


def _pack_pairs(x):
    half = x.shape[1] // 2
    lo = pltpu.bitcast(x[:, :half].astype(BF16).astype(F32), U32) >> 16
    hi = pltpu.bitcast(x[:, half:].astype(BF16).astype(F32), U32) & jnp.uint32(0xFFFF0000)
    return hi | lo


def _unpack_pairs(w):
    lo = pltpu.bitcast(w << 16, F32)
    hi = pltpu.bitcast(w & jnp.uint32(0xFFFF0000), F32)
    return jnp.concatenate([lo, hi], axis=1)


def _ln_pack_kernel(alpha, h_ref, z_ref, gate_ref, lng_ref, lnb_ref, sc_ref, sh_ref, hn_ref, vp_ref):
    y = _layer_norm_rows(alpha * h_ref[...] + gate_ref[...] * z_ref[...], lng_ref[...], lnb_ref[...])
    hn_ref[...] = y
    vp_ref[...] = _pack_pairs(y * (1.0 + sc_ref[...]) + sh_ref[...])


def ln_residual_pack(rows, h, z, modv, layer, lng, lnb, alpha):
    tm = rows.tm // 2
    d = rows.d
    vec = pl.BlockSpec((1, d), lambda i: (0, 0))
    return pl.pallas_call(
        functools.partial(_ln_pack_kernel, alpha),
        grid=(rows.t // tm,),
        in_specs=[rows.row_spec(tm), rows.row_spec(tm), rows.mod_spec(layer, 2, tm), vec, vec,
                  rows.mod_spec(layer, 4, tm), rows.mod_spec(layer, 3, tm)],
        out_specs=[rows.row_spec(tm), rows.row_spec(tm, d // 2)],
        out_shape=[jax.ShapeDtypeStruct((rows.t, d), F32), jax.ShapeDtypeStruct((rows.t, d // 2), U32)],
        compiler_params=_cparams(("parallel",)),
        name="ln_residual_pack",
    )(h, z, modv, lng.reshape(1, d), lnb.reshape(1, d), modv, modv)


def _router_kernel(tm, ne, h_ref, sc_ref, sh_ref, w_ref, bias_ref, eid_ref, pos_ref, wgt_ref, cnt_ref, run_ref):
    i = pl.program_id(0)

    @pl.when(i == 0)
    def _():
        run_ref[...] = jnp.zeros_like(run_ref)

    x = h_ref[...] * (1.0 + sc_ref[...]) + sh_ref[...]
    scores = _sigmoid(_fdot(x, w_ref[...]))
    biased = scores + bias_ref[...]
    lane = lax.broadcasted_iota(I32, (tm, ne), 1)
    gsz = ne // N_GROUPS
    grp = lane // gsz
    neg = -jnp.inf

    gs = []
    for g in range(N_GROUPS):
        m = jnp.where(grp == g, biased, neg)
        top1 = jnp.max(m, axis=1, keepdims=True)
        n_top = jnp.sum((m == top1).astype(F32), axis=1, keepdims=True)
        rest = jnp.max(jnp.where(m < top1, m, neg), axis=1, keepdims=True)
        gs.append(top1 + jnp.where(n_top >= 2.0, top1, rest))
    emask = jnp.zeros((tm, ne), jnp.bool_)
    for g in range(N_GROUPS):
        beaten = jnp.zeros((tm, 1), F32)
        for g2 in range(N_GROUPS):
            if g2 == g:
                continue
            wins = (gs[g2] > gs[g]) | (gs[g2] == gs[g]) if g2 < g else gs[g2] > gs[g]
            beaten = beaten + wins.astype(F32)
        emask = emask | ((grp == g) & (beaten < TOPK_GROUPS))
    masked = jnp.where(emask, biased, neg)

    sel = jnp.zeros((tm, ne), jnp.bool_)
    idxs = []
    lane_f = lane.astype(F32)
    for _ in range(TOP_K):
        mx = jnp.max(masked, axis=1, keepdims=True)
        idx = jnp.min(jnp.where(masked == mx, lane_f, float(ne)), axis=1, keepdims=True).astype(I32)
        hit = lane == idx
        sel = sel | hit
        masked = jnp.where(hit, neg, masked)
        idxs.append(idx)
    w_sel = jnp.where(sel, scores, 0.0)
    w_sel = w_sel / jnp.sum(w_sel, axis=1, keepdims=True) * ROUTED_SCALE

    ri = lax.broadcasted_iota(I32, (tm, tm), 0)
    ci = lax.broadcasted_iota(I32, (tm, tm), 1)
    self = sel.astype(F32)
    before = _bdot((ri > ci).astype(F32), self) + run_ref[0:1, :]
    run_ref[0:1, :] = run_ref[0:1, :] + jnp.sum(self, axis=0, keepdims=True)
    cnt_ref[...] = jnp.broadcast_to(run_ref[0:1, :], cnt_ref.shape)

    lane_o = lax.broadcasted_iota(I32, (tm, LANES), 1)
    eid = jnp.zeros((tm, LANES), I32)
    pos = jnp.zeros((tm, LANES), I32)
    wgt = jnp.zeros((tm, LANES), F32)
    for kk, idx in enumerate(idxs):
        hit = lane == idx
        p_k = jnp.sum(jnp.where(hit, before, 0.0), axis=1, keepdims=True).astype(I32)
        w_k = jnp.sum(jnp.where(hit, w_sel, 0.0), axis=1, keepdims=True)
        eid = jnp.where(lane_o == kk, idx, eid)
        pos = jnp.where(lane_o == kk, p_k, pos)
        wgt = jnp.where(lane_o == kk, w_k, wgt)
    eid_ref[...] = eid
    pos_ref[...] = pos
    wgt_ref[...] = wgt


def router(rows, h, modv, layer, router_w, router_bias):
    t, d = h.shape
    ne = router_w.shape[1]
    tm = rows.tm // 2
    row = pl.BlockSpec((tm, LANES), lambda i: (i, 0))
    return pl.pallas_call(
        functools.partial(_router_kernel, tm, ne),
        grid=(t // tm,),
        in_specs=[rows.row_spec(tm), rows.mod_spec(layer, 4, tm), rows.mod_spec(layer, 3, tm),
                  pl.BlockSpec((d, ne), lambda i: (0, 0)), pl.BlockSpec((1, ne), lambda i: (0, 0))],
        out_specs=[row, row, row, pl.BlockSpec((8, ne), lambda i: (0, 0))],
        out_shape=[jax.ShapeDtypeStruct((t, LANES), I32), jax.ShapeDtypeStruct((t, LANES), I32),
                   jax.ShapeDtypeStruct((t, LANES), F32), jax.ShapeDtypeStruct((8, ne), F32)],
        scratch_shapes=[pltpu.VMEM((8, ne), F32)],
        compiler_params=_cparams(("arbitrary",)),
        name="router",
    )(h, modv, modv, router_w, router_bias.reshape(1, ne))


def _dispatch_kernel(tm, ne, n_all_tiles, pstart_ref, npad_ref, nt_ref, dest_ref, x_ref, xs_ref, zero_ref, sem, sem_pad):
    def pad_copy(e, r):
        return pltpu.make_async_copy(zero_ref.at[pl.ds(0, 1), :], xs_ref.at[pl.ds(pstart_ref[e] + r, 1), :], sem_pad)

    def tail_copy(j):
        return pltpu.make_async_copy(zero_ref, xs_ref.at[pl.ds(pl.multiple_of(j * TME, TME), TME), :], sem_pad)

    @pl.when(pl.program_id(0) == 0)
    def _():
        zero_ref[...] = jnp.zeros_like(zero_ref)

        def fill(e, c):
            lax.fori_loop(0, npad_ref[e], lambda r, c2: (pad_copy(e, r).start(), c2)[1], 0)
            return c

        lax.fori_loop(0, ne, fill, 0)
        lax.fori_loop(nt_ref[0], n_all_tiles, lambda j, c: (tail_copy(j).start(), c)[1], 0)

        def drain(e, c):
            lax.fori_loop(0, npad_ref[e], lambda r, c2: (pad_copy(e, r).wait(), c2)[1], 0)
            return c

        lax.fori_loop(0, ne, drain, 0)
        lax.fori_loop(nt_ref[0], n_all_tiles, lambda j, c: (tail_copy(j).wait(), c)[1], 0)

    def issue(t, c):
        for kk in range(TOP_K):
            pltpu.make_async_copy(x_ref.at[pl.ds(t, 1), :], xs_ref.at[pl.ds(dest_ref[0, t * TOP_K + kk], 1), :], sem).start()
        return c

    lax.fori_loop(0, tm, issue, 0)
    for _ in range(TOP_K):
        pltpu.make_async_copy(x_ref, x_ref, sem).wait()


def dispatch(vp, dest, pad_start, n_pad, n_tiles, n_rows):
    t, dh = vp.shape
    ne = pad_start.shape[0]
    tm = _pick_tile(t, (256, 128))
    gs = pltpu.PrefetchScalarGridSpec(
        num_scalar_prefetch=3,
        grid=(t // tm,),
        in_specs=[pl.BlockSpec((None, 1, tm * TOP_K), lambda i, ps, npd, nt: (i, 0, 0), memory_space=pltpu.SMEM),
                  pl.BlockSpec((tm, dh), lambda i, ps, npd, nt: (i, 0))],
        out_specs=pl.BlockSpec(memory_space=pl.ANY),
        scratch_shapes=[pltpu.VMEM((TME, dh), U32), pltpu.SemaphoreType.DMA, pltpu.SemaphoreType.DMA],
    )
    return pl.pallas_call(
        functools.partial(_dispatch_kernel, tm, ne, n_rows // TME),
        grid_spec=gs,
        out_shape=jax.ShapeDtypeStruct((n_rows, dh), U32),
        compiler_params=_cparams(("arbitrary",)),
        name="moe_dispatch",
    )(pad_start, n_pad, n_tiles, dest.reshape(t // tm, 1, tm * TOP_K), vp)


def _swiglu_kernel(eh, pack_out, te_ref, nt_ref, x_ref, wi_ref, wo_ref, o_ref, wib_ref, wob_ref):
    j = pl.program_id(0)

    @pl.when(j < nt_ref[0])
    def _():
        @pl.when((j == 0) | (te_ref[j] != te_ref[jnp.maximum(j - 1, 0)]))
        def _():
            wib_ref[...] = wi_ref[...].astype(wib_ref.dtype)
            wob_ref[...] = wo_ref[...].astype(wob_ref.dtype)

        ab = _bdot(_unpack_pairs(x_ref[...]), wib_ref[...])
        act = _silu(ab[:, :eh]) * ab[:, eh:]
        y = _bdot(act, wob_ref[...])
        o_ref[...] = _pack_pairs(y) if pack_out else y

    @pl.when(j >= nt_ref[0])
    def _():
        o_ref[...] = jnp.zeros_like(o_ref)


def grouped_swiglu(xs, w_in, w_out, layer, tile_expert, n_tiles, tile, pack_out):
    r, dh = xs.shape
    _, ne, d, eh2 = w_in.shape
    eh = eh2 // 2
    grid_n = r // tile
    out_w, out_dtype = (dh, U32) if pack_out else (d, F32)

    def row_map(j, te, nt):
        return (jnp.minimum(j, nt[0] - 1), 0)

    gs = pltpu.PrefetchScalarGridSpec(
        num_scalar_prefetch=2,
        grid=(grid_n,),
        in_specs=[pl.BlockSpec((tile, dh), row_map),
                  pl.BlockSpec((None, None, d, eh2), lambda j, te, nt: (layer, te[j], 0, 0)),
                  pl.BlockSpec((None, None, eh, d), lambda j, te, nt: (layer, te[j], 0, 0))],
        out_specs=pl.BlockSpec((tile, out_w), lambda j, te, nt: (j, 0)),
        scratch_shapes=[pltpu.VMEM((d, eh2), BF16), pltpu.VMEM((eh, d), BF16)],
    )
    return pl.pallas_call(
        functools.partial(_swiglu_kernel, eh, pack_out),
        grid_spec=gs,
        out_shape=jax.ShapeDtypeStruct((r, out_w), out_dtype),
        compiler_params=_cparams(("arbitrary",)),
        name="grouped_swiglu",
    )(tile_expert, n_tiles, xs, w_in, w_out)


def _combine_kernel(tm, alpha, has_next, dest_ref, destn_ref, wgt_ref, sh_ref, h_ref, gate_ref, lng_ref, lnb_ref,
                    sc_ref, sh2_ref, ys_ref, *rest):
    if has_next:
        hn_ref, u_ref, buf_ref, sem = rest
    else:
        hn_ref, buf_ref, sem = rest
    i = pl.program_id(0)
    slot = i % 2

    def gather(d_ref, s):
        def body(t, c):
            for kk in range(TOP_K):
                pltpu.make_async_copy(ys_ref.at[pl.ds(d_ref[0, t * TOP_K + kk], 1), :],
                                      buf_ref.at[s, kk, pl.ds(t, 1), :], sem.at[s]).start()
            return c
        lax.fori_loop(0, tm, body, 0)

    @pl.when(i == 0)
    def _():
        gather(dest_ref, 0)

    @pl.when(i + 1 < pl.num_programs(0))
    def _():
        gather(destn_ref, 1 - slot)

    pltpu.make_async_copy(buf_ref.at[slot], buf_ref.at[slot], sem.at[slot]).wait()

    wgt = wgt_ref[...]
    f = sh_ref[...]
    for kk in range(TOP_K):
        f = f + _lane_col(wgt, kk) * _unpack_pairs(buf_ref[slot, kk])
    y = _layer_norm_rows(alpha * h_ref[...] + gate_ref[...] * f, lng_ref[...], lnb_ref[...])
    hn_ref[...] = y
    if has_next:
        u_ref[...] = (y * (1.0 + sc_ref[...]) + sh2_ref[...]).astype(u_ref.dtype)


def moe_combine_ln(rows, ys, dest, wgt, shared, h, modv, layer, lng, lnb, alpha, has_next):
    tm = 128
    t, d = rows.t, rows.d
    dh = ys.shape[1]
    n_all = t // tm
    off = 0 if has_next else rows.n_ctx // tm
    n_out = n_all - off
    vec = pl.BlockSpec((1, d), lambda i: (0, 0))
    nl = layer + 1 if has_next else layer

    def row(width):
        return pl.BlockSpec((tm, width), lambda i: (i + off, 0))

    def mod(lyr, which):
        return pl.BlockSpec((None, 1, d), lambda i: ((lyr * MOD_ROWS + rows.seg(i + off, tm)) * 6 + which, 0, 0))

    out_specs = [pl.BlockSpec((tm, d), lambda i: (i, 0))]
    out_shape = [jax.ShapeDtypeStruct((n_out * tm, d), F32)]
    if has_next:
        out_specs.append(pl.BlockSpec((tm, d), lambda i: (i, 0)))
        out_shape.append(jax.ShapeDtypeStruct((n_out * tm, d), BF16))
    dest3 = dest.reshape(n_all, 1, tm * TOP_K)
    return pl.pallas_call(
        functools.partial(_combine_kernel, tm, alpha, has_next),
        grid=(n_out,),
        in_specs=[pl.BlockSpec((None, 1, tm * TOP_K), lambda i: (i + off, 0, 0), memory_space=pltpu.SMEM),
                  pl.BlockSpec((None, 1, tm * TOP_K), lambda i: (jnp.minimum(i + off + 1, n_all - 1), 0, 0),
                               memory_space=pltpu.SMEM),
                  row(LANES), row(d), row(d), mod(layer, 5), vec, vec, mod(nl, 1), mod(nl, 0),
                  pl.BlockSpec(memory_space=pl.ANY)],
        out_specs=out_specs,
        out_shape=out_shape,
        scratch_shapes=[pltpu.VMEM((2, TOP_K, tm, dh), U32), pltpu.SemaphoreType.DMA((2,))],
        compiler_params=_cparams(("arbitrary",)),
        name="moe_combine_ln",
    )(dest3, dest3, wgt, shared, h, modv, lng.reshape(1, d), lnb.reshape(1, d), modv, modv, ys)


def moe_ffn(rows, vp, h, modv, layer, lng, lnb, alpha, has_next, router_w, router_bias, exp_w_in, exp_w_out,
            sh_w_in, sh_w_out):
    t = vp.shape[0]
    ne = router_w.shape[1]
    eid, pos, wgt, cnt = router(rows, h, modv, layer, router_w, router_bias)
    counts = cnt[0].astype(I32)
    padded = (counts + TME - 1) // TME * TME
    ends = jnp.cumsum(padded)
    offs = ends - padded
    dest = offs[eid[:, :TOP_K]] + pos[:, :TOP_K]
    n_rows = (t * TOP_K // TME + ne) * TME
    tile_start = jnp.arange(n_rows // TME, dtype=I32) * TME
    tile_expert = jnp.minimum(jnp.sum((tile_start[:, None] >= ends[None, :]).astype(I32), axis=1), ne - 1)
    n_tiles = (ends[-1] // TME).reshape(1)

    xs = dispatch(vp, dest, offs + counts, padded - counts, n_tiles, n_rows)
    ys = grouped_swiglu(xs, exp_w_in, exp_w_out, layer, tile_expert, n_tiles, TME, True)
    st = rows.tm // 2
    shared = grouped_swiglu(vp, sh_w_in[:, None], sh_w_out[:, None], layer, jnp.zeros((t // st,), I32),
                            jnp.full((1,), t // st, I32), st, False)
    return moe_combine_ln(rows, ys, dest, wgt, shared, h, modv, layer, lng, lnb, alpha, has_next)


def _pad_cols(w, n):
    return jnp.pad(w, ((0, 0), (0, n - w.shape[1])))


def _pad_row(v, n=LANES):
    v = v.reshape(1, -1).astype(F32)
    return jnp.pad(v, ((0, 0), (0, n - v.shape[1])))


def retention_mlstm_mixer(rows, order, u, w_in, w_out, ret_decay_rate, ret_norm_g, ml_gate_b, ml_norm_g):
    nh_r = ret_decay_rate.shape[-1]
    nh_m = ml_gate_b.shape[-1]
    ret_w = ret_norm_g.shape[0]
    ml_w = ml_norm_g.shape[0]
    hd_r, hd_m = ret_w // nh_r, ml_w // nh_m
    assert hd_r == hd_m
    main_w = 4 * ret_w + 4 * ml_w
    wb = w_in.astype(BF16)
    p = matmul(u, wb[:, :main_w], name="ab_in_proj")
    pg = matmul(u, _pad_cols(wb[:, main_w:], LANES), name="ab_gate_proj")

    pos = jnp.arange(rows.ctx + rows.seq, dtype=F32)
    half = hd_r // 2
    inv_freq = ROPE_BASE ** (-jnp.arange(half, dtype=F32) / half)
    ang = pos[:, None] * inv_freq[None, :]
    rate = jnp.broadcast_to(ret_decay_rate.astype(F32).reshape(2 * nh_r, 1, 1), (2 * nh_r, 1, LANES))
    y_r = retention(order, p, jnp.cos(ang), jnp.sin(ang), rate, ret_norm_g, nh_r, hd_r)

    gc, gr = ml_gates(pg, _pad_row(ml_gate_b), nh_m)
    y_m = mlstm(order, p, 4 * nh_r, gc, gr, ml_norm_g, nh_m, hd_m)
    return matmul2(y_r, y_m, w_out.astype(BF16), name="ab_out_proj")


def gated_deltanet_mixer(rows, order, u, w_in, conv_w, a_log, dt_bias, norm_g, w_out):
    nh = a_log.shape[-1]
    hd = norm_g.shape[0]
    dn_w = nh * hd
    wb = w_in.astype(BF16)
    p = matmul(u, wb[:, :4 * dn_w], name="dn_in_proj")
    pg = matmul(u, _pad_cols(wb[:, 4 * dn_w:], LANES), name="dn_gate_proj")
    conv_w9 = jnp.pad(conv_w.reshape(9, 3 * dn_w).astype(F32), ((0, 7), (0, 0)))
    qkv = dn_conv(rows, p, conv_w9, dn_w, hd)
    gc, gr, gt = dn_gates(pg, _pad_row(a_log), _pad_row(dt_bias), nh)
    y = gated_deltanet(order, qkv, p, gc, gr, gt, norm_g, nh, hd)
    return matmul(y, w_out.astype(BF16), name="dn_out_proj")


def kernel(x, c, ctx, c_ctx, ada_w, ada_b, ln_g, ln_b, ab_w_in, ab_w_out, ret_decay_rate, ret_norm_g, ml_gate_b, ml_norm_g, dn_w_in, dn_conv_w, dn_a_log, dn_dt_bias, dn_norm_g, dn_w_out, router_w, router_bias, exp_w_in, exp_w_out, sh_w_in, sh_w_out):
    b, seq, d = x.shape
    n_ctx = ctx.shape[1]
    depth = ada_w.shape[0]
    alpha = (2 * depth) ** 0.25
    rows = Rows(b, n_ctx, seq, d)
    order = ScanOrder(b, n_ctx, seq)
    assert b + 1 <= MOD_ROWS

    cvec = jnp.zeros((MOD_ROWS, d), F32).at[:b].set(c).at[b].set(c_ctx)
    modv = ada_table(cvec, ada_w, ada_b).reshape(depth * MOD_ROWS * 6, 1, d)

    h = jnp.concatenate([ctx.reshape(b * n_ctx, d), x.reshape(b * seq, d)], axis=0)
    u = modulate(rows, h, modv, 0, 1, 0)
    for layer in range(depth):
        j = layer // 2
        if layer % 2 == 0:
            z = retention_mlstm_mixer(rows, order, u, ab_w_in[j], ab_w_out[j], ret_decay_rate[j], ret_norm_g[j],
                                      ml_gate_b[j], ml_norm_g[j])
        else:
            z = gated_deltanet_mixer(rows, order, u, dn_w_in[j], dn_conv_w[j], dn_a_log[j], dn_dt_bias[j],
                                     dn_norm_g[j], dn_w_out[j])
        h, vp = ln_residual_pack(rows, h, z, modv, layer, ln_g[layer, 0], ln_b[layer, 0], alpha)
        has_next = layer < depth - 1
        outs = moe_ffn(rows, vp, h, modv, layer, ln_g[layer, 1], ln_b[layer, 1], alpha, has_next, router_w[layer],
                       router_bias[layer], exp_w_in, exp_w_out, sh_w_in, sh_w_out)
        if has_next:
            h, u = outs
        else:
            h = outs[0]
    return h.reshape(b, seq, d)
```

```python
import functools
import math

import jax
import jax.numpy as jnp
from jax import lax
from jax.experimental import pallas as pl
from jax.experimental.pallas import tpu as pltpu

F32 = jnp.float32
BF16 = jnp.bfloat16
I32 = jnp.int32
U32 = jnp.uint32

V7X_VMEM_BYTES = 64 * 1024 * 1024
VMEM_LIMIT = V7X_VMEM_BYTES - 8 * 1024 * 1024
LANES = 128

GRID_W = 64
CHUNK = 64
TOP_K = 8
N_GROUPS = 8
TOPK_GROUPS = 4
ROUTED_SCALE = 2.5
ROPE_BASE = 10000.0
LN_EPS = 1e-5

LB = 256
SCAN_HG = 2
TME = 512
MOD_ROWS = 8


def _cparams(sem):
    return pltpu.CompilerParams(dimension_semantics=sem, vmem_limit_bytes=VMEM_LIMIT)


def _bdot(a, b):
    return jnp.dot(a.astype(BF16), b.astype(BF16), preferred_element_type=F32)


def _bdot_nt(a, b):
    return lax.dot_general(a.astype(BF16), b.astype(BF16), (((1,), (1,)), ((), ())), preferred_element_type=F32)


def _bdot_tn(a, b):
    return lax.dot_general(a.astype(BF16), b.astype(BF16), (((0,), (0,)), ((), ())), preferred_element_type=F32)


def _fdot(a, b):
    return jnp.dot(a, b, preferred_element_type=F32, precision=lax.Precision.HIGHEST)


def _sigmoid(x):
    return 1.0 / (1.0 + jnp.exp(-x))


def _silu(x):
    return x * _sigmoid(x)


def _softplus(x):
    return jnp.maximum(x, 0.0) + jnp.log(1.0 + jnp.exp(-jnp.abs(x)))


def _log_sigmoid(x):
    return -_softplus(-x)


def _lane_col(x, lane):
    idx = lax.broadcasted_iota(I32, x.shape, 1)
    return jnp.sum(jnp.where(idx == lane, x, 0.0), axis=1, keepdims=True)


def _ada_kernel(c_ref, w_ref, b_ref, o_ref):
    o_ref[...] = _bdot(_silu(c_ref[...]), w_ref[...]) + b_ref[...]


def ada_table(cvec, ada_w, ada_b):
    depth, d, n = ada_w.shape
    tn = 512
    return pl.pallas_call(
        _ada_kernel,
        grid=(depth, n // tn),
        in_specs=[pl.BlockSpec((MOD_ROWS, d), lambda l, j: (0, 0)),
                  pl.BlockSpec((None, d, tn), lambda l, j: (l, 0, j)),
                  pl.BlockSpec((None, 1, tn), lambda l, j: (l, 0, j))],
        out_specs=pl.BlockSpec((None, MOD_ROWS, tn), lambda l, j: (l, 0, j)),
        out_shape=jax.ShapeDtypeStruct((depth, MOD_ROWS, n), F32),
        compiler_params=_cparams(("parallel", "parallel")),
        name="ada_table",
    )(cvec, ada_w, ada_b.reshape(depth, 1, n))


class Rows:
    def __init__(self, b, ctx, seq, d):
        self.b, self.ctx, self.seq, self.d = b, ctx, seq, d
        self.n_ctx = b * ctx
        self.t = b * ctx + b * seq
        self.tm = math.gcd(math.gcd(self.n_ctx, seq), 512)

    def seg(self, i, tm):
        n_ctx_tiles = self.n_ctx // tm
        return jnp.where(i < n_ctx_tiles, self.b, (i - n_ctx_tiles) // (self.seq // tm))

    def mod_spec(self, layer, which, tm):
        return pl.BlockSpec((None, 1, self.d), lambda i: ((layer * MOD_ROWS + self.seg(i, tm)) * 6 + which, 0, 0))

    def row_spec(self, tm, width=None):
        return pl.BlockSpec((tm, self.d if width is None else width), lambda i: (i, 0))

    def src_specs(self, tm):
        nct = self.n_ctx // tm
        return [pl.BlockSpec((tm, self.d), lambda i: (jnp.minimum(i, nct - 1), 0)),
                pl.BlockSpec((tm, self.d), lambda i: (jnp.maximum(i - nct, 0), 0))]


def _mod_kernel(n_ctx_tiles, c_ref, x_ref, sc_ref, sh_ref, o_ref):
    h = jnp.where(pl.program_id(0) < n_ctx_tiles, c_ref[...], x_ref[...])
    o_ref[...] = (h * (1.0 + sc_ref[...]) + sh_ref[...]).astype(o_ref.dtype)


def modulate(rows, ctx2d, x2d, modv, layer, sc_idx, sh_idx):
    tm = rows.tm
    return pl.pallas_call(
        functools.partial(_mod_kernel, rows.n_ctx // tm),
        grid=(rows.t // tm,),
        in_specs=rows.src_specs(tm) + [rows.mod_spec(layer, sc_idx, tm), rows.mod_spec(layer, sh_idx, tm)],
        out_specs=rows.row_spec(tm),
        out_shape=jax.ShapeDtypeStruct((rows.t, rows.d), BF16),
        compiler_params=_cparams(("parallel",)),
        name="modulate",
    )(ctx2d, x2d, modv, modv)


def _layer_norm_rows(r, g, b):
    mu = jnp.mean(r, axis=-1, keepdims=True)
    xc = r - mu
    var = jnp.mean(xc * xc, axis=-1, keepdims=True)
    return xc * lax.rsqrt(var + LN_EPS) * g + b


def _mm_kernel(x_ref, w_ref, o_ref):
    o_ref[...] = _bdot(x_ref[...], w_ref[...]).astype(o_ref.dtype)


def _pick_tile(n, prefs):
    for p in prefs:
        if n % p == 0:
            return p
    return n


def matmul(x, w, out_dtype=F32, name="matmul"):
    m, k = x.shape
    n = w.shape[1]
    tm = _pick_tile(m, (768, 512, 256, 128))
    tn = _pick_tile(n, (1024, 512, 256, 128))
    return pl.pallas_call(
        _mm_kernel,
        grid=(n // tn, m // tm),
        in_specs=[pl.BlockSpec((tm, k), lambda j, i: (i, 0)), pl.BlockSpec((k, tn), lambda j, i: (0, j))],
        out_specs=pl.BlockSpec((tm, tn), lambda j, i: (i, j)),
        out_shape=jax.ShapeDtypeStruct((m, n), out_dtype),
        compiler_params=_cparams(("parallel", "parallel")),
        name=name,
    )(x, w)


def _mm2_kernel(x1_ref, x2_ref, w1_ref, w2_ref, o_ref):
    o_ref[...] = _bdot(x1_ref[...], w1_ref[...]) + _bdot(x2_ref[...], w2_ref[...])


def matmul2(x1, x2, w, name="matmul2"):
    m, k1 = x1.shape
    k2 = x2.shape[1]
    n = w.shape[1]
    tm = _pick_tile(m, (768, 512, 256, 128))
    tn = _pick_tile(n, (1024, 512, 256, 128))
    return pl.pallas_call(
        _mm2_kernel,
        grid=(n // tn, m // tm),
        in_specs=[pl.BlockSpec((tm, k1), lambda j, i: (i, 0)), pl.BlockSpec((tm, k2), lambda j, i: (i, 0)),
                  pl.BlockSpec((k1, tn), lambda j, i: (0, j)), pl.BlockSpec((k2, tn), lambda j, i: (k1 // k2, j))],
        out_specs=pl.BlockSpec((tm, tn), lambda j, i: (i, j)),
        out_shape=jax.ShapeDtypeStruct((m, n), F32),
        compiler_params=_cparams(("parallel", "parallel")),
        name=name,
    )(x1, x2, w, w)


class ScanOrder:
    def __init__(self, b, ctx, seq):
        assert ctx % LB == 0 and seq % LB == 0
        self.b = b
        self.nc = ctx // LB
        self.nl = seq // LB
        self.steps = self.nc + self.nl

    def row_block(self, direction, bi, s):
        nc, nl = self.nc, self.nl
        if direction == 0:
            c, l = s, s - nc
        else:
            c, l = nc - 1 - s, nl - 1 - (s - nc)
        return jnp.where(s < nc, bi * nc + c, self.b * nc + bi * nl + l)

    def pos_block(self, direction, s):
        nc, nl = self.nc, self.nl
        if direction == 0:
            return s
        return jnp.where(s < nc, nc - 1 - s, nc + nl - 1 - (s - nc))


def _causal(direction, n):
    ri = lax.broadcasted_iota(I32, (n, n), 0)
    ci = lax.broadcasted_iota(I32, (n, n), 1)
    rel = ri - ci if direction == 0 else ci - ri
    return rel >= 0, rel > 0, rel


def _rope(t, cos, sin):
    half = t.shape[-1] // 2
    t1, t2 = t[:, :half], t[:, half:]
    return jnp.concatenate([t1 * cos - t2 * sin, t1 * sin + t2 * cos], axis=-1)


def _head_ln_gate(o, g, gate_act):
    mu = jnp.mean(o, axis=-1, keepdims=True)
    xc = o - mu
    var = jnp.mean(xc * xc, axis=-1, keepdims=True)
    return xc * lax.rsqrt(var + LN_EPS) * g * gate_act


def _ret_kernel(direction, hd, *refs):
    ng = SCAN_HG
    s_refs = refs[-ng:]
    if direction == 0:
        q_ref, k_ref, v_ref, cos_ref, sin_ref, rate_ref, o_ref = refs[:-ng]
    else:
        q_ref, k_ref, v_ref, cos_ref, sin_ref, rate_ref, of_ref, g_ref, ng_ref, o_ref = refs[:-ng]
    step = pl.program_id(2)

    @pl.when(step == 0)
    def _():
        for s_ref in s_refs:
            s_ref[...] = jnp.zeros_like(s_ref)

    incl, _, rel = _causal(direction, LB)
    relf = jnp.maximum(rel, 0).astype(F32)
    ri = lax.broadcasted_iota(I32, (LB, 1), 0).astype(F32)
    q_pow, k_pow = (ri + 1.0, LB - 1.0 - ri) if direction == 0 else (LB - ri, ri)
    cos, sin = cos_ref[...], sin_ref[...]
    heads = range(ng)
    sls = [slice(hh * hd, (hh + 1) * hd) for hh in heads]
    lgs = [-jnp.exp(rate_ref[hh])[:, :1] for hh in heads]
    qs = [_rope(q_ref[:, sl], cos, sin) for sl in sls]
    ks = [_rope(k_ref[:, sl], cos, sin) * hd ** -0.5 for sl in sls]
    vs = [v_ref[:, sl] for sl in sls]
    ss = [s_ref[...] for s_ref in s_refs]
    atts = [_bdot_nt(q, k) * jnp.where(incl, jnp.exp(lg * relf), 0.0) for q, k, lg in zip(qs, ks, lgs)]
    inter = [_bdot(q * jnp.exp(lg * q_pow), s) for q, lg, s in zip(qs, lgs, ss)]
    outs = [_bdot(att, v) + oi for att, v, oi in zip(atts, vs, inter)]
    for hh in heads:
        lg = lgs[hh]
        s_refs[hh][...] = ss[hh] * jnp.exp(lg * LB) + _bdot_tn(ks[hh] * jnp.exp(lg * k_pow), vs[hh])
    for hh in heads:
        sl = sls[hh]
        if direction == 0:
            o_ref[:, sl] = outs[hh]
        else:
            o_ref[:, sl] = _head_ln_gate(outs[hh] + of_ref[:, sl], ng_ref[:, sl], _silu(g_ref[:, sl])).astype(o_ref.dtype)


def retention(order, p, cos, sin, rate, norm_g, nh, hd):
    t = p.shape[0]
    b = order.b
    half = hd // 2
    g = SCAN_HG
    assert nh % g == 0
    gw = g * hd

    def run(direction, extra_in, extra_specs, out_dtype):
        def blk(col0):
            return pl.BlockSpec((LB, gw), lambda bi, h, s: (order.row_block(direction, bi, s), col0 // g + h))
        in_specs = [blk(0), blk(nh), blk(2 * nh),
                    pl.BlockSpec((LB, half), lambda bi, h, s: (order.pos_block(direction, s), 0)),
                    pl.BlockSpec((LB, half), lambda bi, h, s: (order.pos_block(direction, s), 0)),
                    pl.BlockSpec((g, 1, LANES), lambda bi, h, s: (direction * nh // g + h, 0, 0))] + extra_specs
        return pl.pallas_call(
            functools.partial(_ret_kernel, direction, hd),
            grid=(b, nh // g, order.steps),
            in_specs=in_specs,
            out_specs=pl.BlockSpec((LB, gw), lambda bi, h, s: (order.row_block(direction, bi, s), h)),
            out_shape=jax.ShapeDtypeStruct((t, nh * hd), out_dtype),
            scratch_shapes=[pltpu.VMEM((hd, hd), F32)] * g,
            compiler_params=_cparams(("parallel", "parallel", "arbitrary")),
            name=f"retention_d{direction}",
        )(p, p, p, cos, sin, rate, *extra_in)

    o_fwd = run(0, [], [], F32)
    return run(1, [o_fwd, p, norm_g.reshape(1, nh * hd)],
               [pl.BlockSpec((LB, gw), lambda bi, h, s: (order.row_block(1, bi, s), h)),
                pl.BlockSpec((LB, gw), lambda bi, h, s: (order.row_block(1, bi, s), 3 * nh // g + h)),
                pl.BlockSpec((1, gw), lambda bi, h, s: (0, h))], BF16)


def _ml_gate_kernel(nh, g_ref, b_ref, gc_ref, gr_ref):
    g = g_ref[...] + b_ref[...]
    ls = _log_sigmoid(g)
    incl_f, _, _ = _causal(0, LB)
    incl_b, _, _ = _causal(1, LB)
    pre = _fdot(incl_f.astype(F32), ls)
    suf = _fdot(incl_b.astype(F32), ls)
    lane = lax.broadcasted_iota(I32, g.shape, 1)
    is_f = (lane // nh) % 2 == 1
    is_d1 = lane // (2 * nh) == 1
    out = jnp.where(is_f, jnp.where(is_d1, suf, pre), g)
    gc_ref[...] = out
    gr_ref[...] = out.T


def ml_gates(pg, bias_row, nh):
    t = pg.shape[0]
    return pl.pallas_call(
        functools.partial(_ml_gate_kernel, nh),
        grid=(t // LB,),
        in_specs=[pl.BlockSpec((LB, LANES), lambda i: (i, 0)), pl.BlockSpec((1, LANES), lambda i: (0, 0))],
        out_specs=[pl.BlockSpec((LB, LANES), lambda i: (i, 0)), pl.BlockSpec((None, LANES, LB), lambda i: (i, 0, 0))],
        out_shape=[jax.ShapeDtypeStruct((t, LANES), F32), jax.ShapeDtypeStruct((t // LB, LANES, LB), F32)],
        compiler_params=_cparams(("parallel",)),
        name="ml_gates",
    )(pg, bias_row)


def _ml_kernel(direction, nh, hd, *refs):
    ng = SCAN_HG
    c_refs, n_refs, m_refs = refs[-3 * ng:-2 * ng], refs[-2 * ng:-ng], refs[-ng:]
    if direction == 0:
        q_ref, k_ref, v_ref, gc_ref, gr_ref, o_ref = refs[:-3 * ng]
    else:
        q_ref, k_ref, v_ref, gc_ref, gr_ref, of_ref, g_ref, ng_ref, o_ref = refs[:-3 * ng]
    grp = pl.program_id(1)
    step = pl.program_id(2)

    @pl.when(step == 0)
    def _():
        for ref in refs[-3 * ng:]:
            ref[...] = jnp.zeros_like(ref)

    gc = gc_ref[...]
    incl, _, _ = _causal(direction, LB)
    last = LB - 1 if direction == 0 else 0
    heads = range(ng)
    sls = [slice(hh * hd, (hh + 1) * hd) for hh in heads]
    lanes_i = [direction * 2 * nh + grp * ng + hh for hh in heads]
    ig_cols = [_lane_col(gc, li) for li in lanes_i]
    b_cols = [_lane_col(gc, li + nh) for li in lanes_i]
    d_logs = [jnp.where(incl, b_col - gr_ref[pl.ds(li + nh, 1), :] + gr_ref[pl.ds(li, 1), :], -jnp.inf)
              for b_col, li in zip(b_cols, lanes_i)]
    inter_logs = [b_col + m_ref[0:1, 0:1] for b_col, m_ref in zip(b_cols, m_refs)]
    m_rows = [jnp.maximum(il, jnp.max(dl, axis=1, keepdims=True)) for il, dl in zip(inter_logs, d_logs)]
    inter_ws = [jnp.exp(il - mr) for il, mr in zip(inter_logs, m_rows)]
    qs = [q_ref[:, sl] for sl in sls]
    ks = [k_ref[:, sl] * hd ** -0.5 for sl in sls]
    vs = [v_ref[:, sl] for sl in sls]
    cs = [c_ref[...] for c_ref in c_refs]
    ns = [n_ref[0:1, :] for n_ref in n_refs]
    ss = [_bdot_nt(q, k) * jnp.exp(dl - mr) for q, k, dl, mr in zip(qs, ks, d_logs, m_rows)]
    qcs = [_bdot(q, c) for q, c in zip(qs, cs)]
    nums = [_bdot(s, v) + iw * qc for s, v, iw, qc in zip(ss, vs, inter_ws, qcs)]
    houts = []
    for hh in heads:
        den = (jnp.sum(ss[hh], axis=1, keepdims=True)
               + inter_ws[hh] * jnp.sum(qs[hh] * ns[hh], axis=1, keepdims=True))
        houts.append(nums[hh] / jnp.maximum(jnp.abs(den), jnp.exp(-m_rows[hh])))
    for hh in heads:
        b_col, m_row = b_cols[hh], m_rows[hh]
        m_last = m_row[last:last + 1, :]
        dec_last = inter_ws[hh][last:last + 1, :]
        w_last = jnp.exp(b_col[last:last + 1, :] - b_col + ig_cols[hh] - m_last)
        kw = ks[hh] * w_last
        c_refs[hh][...] = dec_last * cs[hh] + _bdot_tn(kw, vs[hh])
        n_refs[hh][0:1, :] = dec_last * ns[hh] + jnp.sum(kw, axis=0, keepdims=True)
        m_refs[hh][...] = jnp.broadcast_to(m_last, m_refs[hh].shape)
    for hh in heads:
        sl = sls[hh]
        if direction == 0:
            o_ref[:, sl] = houts[hh]
        else:
            o_ref[:, sl] = _head_ln_gate(houts[hh] + of_ref[:, sl], ng_ref[:, sl],
                                         _sigmoid(g_ref[:, sl])).astype(o_ref.dtype)


def mlstm(order, p, col0, gc, gr, norm_g, nh, hd):
    t = p.shape[0]
    b = order.b
    g = SCAN_HG
    assert nh % g == 0 and col0 % g == 0
    gw = g * hd

    def run(direction, extra_in, extra_specs, out_dtype):
        def blk(c):
            return pl.BlockSpec((LB, gw), lambda bi, h, s: (order.row_block(direction, bi, s), (col0 + c) // g + h))
        in_specs = [blk(0), blk(nh), blk(2 * nh),
                    pl.BlockSpec((LB, LANES), lambda bi, h, s: (order.row_block(direction, bi, s), 0)),
                    pl.BlockSpec((None, LANES, LB), lambda bi, h, s: (order.row_block(direction, bi, s), 0, 0))]
        return pl.pallas_call(
            functools.partial(_ml_kernel, direction, nh, hd),
            grid=(b, nh // g, order.steps),
            in_specs=in_specs + extra_specs,
            out_specs=pl.BlockSpec((LB, gw), lambda bi, h, s: (order.row_block(direction, bi, s), h)),
            out_shape=jax.ShapeDtypeStruct((t, nh * hd), out_dtype),
            scratch_shapes=([pltpu.VMEM((hd, hd), F32)] * g + [pltpu.VMEM((8, hd), F32)] * g
                            + [pltpu.VMEM((8, LANES), F32)] * g),
            compiler_params=_cparams(("parallel", "parallel", "arbitrary")),
            name=f"mlstm_d{direction}",
        )(p, p, p, gc, gr, *extra_in)

    o_fwd = run(0, [], [], F32)
    return run(1, [o_fwd, p, norm_g.reshape(1, nh * hd)],
               [pl.BlockSpec((LB, gw), lambda bi, h, s: (order.row_block(1, bi, s), h)),
                pl.BlockSpec((LB, gw), lambda bi, h, s: (order.row_block(1, bi, s), (col0 + 3 * nh) // g + h)),
                pl.BlockSpec((1, gw), lambda bi, h, s: (0, h))], BF16)


CONV_HALO = 128


def _conv_kernel(tb, tc, n_ctx, ctx, seq, n_qk_tiles, n_q_tiles, hd, xp_ref, xc_ref, xn_ref, w_ref, o_ref, buf_ref, lr_ref):
    i = pl.program_id(0)
    j = pl.program_id(1)
    is_ctx = i * tb < n_ctx
    lat0 = (i * tb - n_ctx) % seq
    top = jnp.logical_not(is_ctx) & (lat0 == 0)
    bottom = jnp.logical_not(is_ctx) & (lat0 == seq - tb)
    buf_ref[0:CONV_HALO, :] = jnp.where(top, 0.0, xp_ref[...])
    buf_ref[CONV_HALO:CONV_HALO + tb, :] = xc_ref[...]
    buf_ref[CONV_HALO + tb:, :] = jnp.where(bottom, 0.0, xn_ref[...])

    ext = tb + 2 * GRID_W
    base = CONV_HALO - GRID_W
    r = lax.broadcasted_iota(I32, (ext, 1), 0)
    col = jnp.where(is_ctx, (r + (ctx - GRID_W % ctx)) % ctx, r % GRID_W)
    n_c = jnp.where(is_ctx, ctx, GRID_W)
    lr_ref[0] = jnp.where(col >= 1, buf_ref[base - 1:base - 1 + ext, :], 0.0)
    lr_ref[1] = jnp.where(col + 1 < n_c, buf_ref[base + 1:base + 1 + ext, :], 0.0)

    acc = None
    for dr in range(3):
        wl, wc, wr = (w_ref[dr * 3 + dc:dr * 3 + dc + 1, :] for dc in range(3))
        if dr != 1:
            wl, wc, wr = (jnp.where(is_ctx, 0.0, w) for w in (wl, wc, wr))
        o0 = CONV_HALO + (dr - 1) * GRID_W
        e0 = dr * GRID_W
        term = (lr_ref[0, e0:e0 + tb, :] * wl + buf_ref[o0:o0 + tb, :] * wc + lr_ref[1, e0:e0 + tb, :] * wr)
        acc = term if acc is None else acc + term
    y = _silu(acc)
    qscale = jnp.where(j < n_q_tiles, hd ** -0.5, 1.0)
    is_qk = j < n_qk_tiles
    for g in range(tc // hd):
        yg = y[:, g * hd:(g + 1) * hd]
        nrm = lax.rsqrt(jnp.sum(yg * yg, axis=1, keepdims=True) + 1e-6) * qscale
        o_ref[:, g * hd:(g + 1) * hd] = yg * jnp.where(is_qk, nrm, 1.0)


def dn_conv(rows, p, conv_w9, dn_w, hd):
    t = rows.t
    tb = rows.tm
    tc = 512
    hb = tb // CONV_HALO
    n_halo = t // CONV_HALO
    kern = functools.partial(_conv_kernel, tb, tc, rows.n_ctx, rows.ctx, rows.seq, 2 * dn_w // tc, dn_w // tc, hd)
    return pl.pallas_call(
        kern,
        grid=(t // tb, 3 * dn_w // tc),
        in_specs=[pl.BlockSpec((CONV_HALO, tc), lambda i, j: (jnp.maximum(i * hb - 1, 0), j)),
                  pl.BlockSpec((tb, tc), lambda i, j: (i, j)),
                  pl.BlockSpec((CONV_HALO, tc), lambda i, j: (jnp.minimum(i * hb + hb, n_halo - 1), j)),
                  pl.BlockSpec((16, tc), lambda i, j: (0, j))],
        out_specs=pl.BlockSpec((tb, tc), lambda i, j: (i, j)),
        out_shape=jax.ShapeDtypeStruct((t, 3 * dn_w), F32),
        scratch_shapes=[pltpu.VMEM((tb + 2 * CONV_HALO, tc), F32), pltpu.VMEM((2, tb + 2 * GRID_W, tc), F32)],
        compiler_params=_cparams(("parallel", "parallel")),
        name="dn_conv",
    )(p, p, p, conv_w9)


def _chunk_masks(direction):
    ri = lax.broadcasted_iota(I32, (LB, LB), 0)
    ci = lax.broadcasted_iota(I32, (LB, LB), 1)
    same = (ri // CHUNK) == (ci // CHUNK)
    rel = ri - ci if direction == 0 else ci - ri
    return same & (rel >= 0), same & (rel > 0), same


def _dn_gate_kernel(nh, ab_ref, alog_ref, dtb_ref, gc_ref, gr_ref, gt_ref):
    x = ab_ref[...]
    lane = lax.broadcasted_iota(I32, x.shape, 1)
    g = -jnp.exp(alog_ref[...]) * _softplus(x + dtb_ref[...])
    g = jnp.where(lane < 2 * nh, g, 0.0)
    incl_f, _, same = _chunk_masks(0)
    incl_b, _, _ = _chunk_masks(1)
    pre = _fdot(incl_f.astype(F32), g)
    suf = _fdot(incl_b.astype(F32), g)
    tot = _fdot(same.astype(F32), g)
    out = jnp.where(lane < nh, pre, jnp.where(lane < 2 * nh, suf, _sigmoid(x)))
    gc_ref[...] = out
    gr_ref[...] = out.T
    gt_ref[...] = tot


def dn_gates(pg, alog_row, dtb_row, nh):
    t = pg.shape[0]
    return pl.pallas_call(
        functools.partial(_dn_gate_kernel, nh),
        grid=(t // LB,),
        in_specs=[pl.BlockSpec((LB, LANES), lambda i: (i, 0)), pl.BlockSpec((1, LANES), lambda i: (0, 0)),
                  pl.BlockSpec((1, LANES), lambda i: (0, 0))],
        out_specs=[pl.BlockSpec((LB, LANES), lambda i: (i, 0)), pl.BlockSpec((None, LANES, LB), lambda i: (i, 0, 0)),
                   pl.BlockSpec((LB, LANES), lambda i: (i, 0))],
        out_shape=[jax.ShapeDtypeStruct((t, LANES), F32), jax.ShapeDtypeStruct((t // LB, LANES, LB), F32),
                   jax.ShapeDtypeStruct((t, LANES), F32)],
        compiler_params=_cparams(("parallel",)),
        name="dn_gates",
    )(pg, alog_row, dtb_row)


DN_HG = 8
DN_SCAN_HG = 8


def _dn_ut_kernel(nh, hd, q_ref, k_ref, v_ref, gc_ref, gr_ref, gt_ref, *refs):
    outs = (refs[0:5], refs[5:10])
    a_ref, t_ref, x_ref, rhs_ref = refs[10:]
    hg = pl.program_id(1)
    ri = lax.broadcasted_iota(I32, (LB, LB), 0)
    ci = lax.broadcasted_iota(I32, (LB, LB), 1)
    eye = (ri == ci).astype(F32)
    same = (ri // CHUNK) == (ci // CHUNK)
    gcb = gc_ref[...]
    gtb = gt_ref[...]
    chains = [(d, hh) for hh in range(DN_HG) for d in (0, 1)]
    for c, (d, hh) in enumerate(chains):
        u_ref, w_ref, qg_ref, kd_ref, att_ref = outs[d]
        rel = ri - ci if d == 0 else ci - ri
        lane_g = d * nh + hg * DN_HG + hh
        gc_col = _lane_col(gcb, lane_g)
        beta = _lane_col(gcb, 2 * nh + lane_g)
        gt_col = _lane_col(gtb, lane_g)
        gc_row = gr_ref[pl.ds(lane_g, 1), :]
        sl = slice(hh * hd, (hh + 1) * hd)
        q, k, v = q_ref[:, sl], k_ref[:, sl], v_ref[:, sl]
        if d == 0:
            kk, qk = _bdot_nt(k, k), _bdot_nt(q, k)
        decay = jnp.exp(jnp.where(same & (rel >= 0), gc_col - gc_row, -jnp.inf))
        kb = k * beta
        a = jnp.where(rel > 0, beta * kk * decay, 0.0)
        a_ref[c] = a
        t_ref[c] = eye - jnp.where((ri // 2) == (ci // 2), a, 0.0)
        eg = jnp.exp(gc_col)
        rhs_ref[c] = jnp.concatenate([v * beta, kb * eg], axis=1).astype(rhs_ref.dtype)
        qg_ref[:, sl] = (q * eg).astype(qg_ref.dtype)
        kd_ref[sl, :] = (k * jnp.exp(gt_col - gc_col)).T.astype(kd_ref.dtype)
        att_ref[:, hh * LB:(hh + 1) * LB] = (qk * decay).astype(att_ref.dtype)
    s = 2
    while s < CHUNK:
        m = ((ri // (2 * s)) == (ci // (2 * s))) & ((ri // s) != (ci // s))
        for c in range(len(chains)):
            x_ref[c] = _bdot(t_ref[c], jnp.where(m, a_ref[c], 0.0)).astype(x_ref.dtype)
        for c in range(len(chains)):
            t = t_ref[c]
            t_ref[c] = t - _bdot(x_ref[c], t)
        s *= 2
    for c, (d, hh) in enumerate(chains):
        u_ref, w_ref = outs[d][0], outs[d][1]
        sl = slice(hh * hd, (hh + 1) * hd)
        sol = _bdot(t_ref[c], rhs_ref[c])
        u_ref[:, sl] = sol[:, :hd]
        w_ref[:, sl] = sol[:, hd:].astype(w_ref.dtype)


def _dn_scan_kernel(direction, nh, hd, hgs, *refs):
    s_refs, vn_refs = refs[-2 * hgs:-hgs], refs[-hgs:]
    if direction == 0:
        u_ref, w_ref, qg_ref, kd_ref, att_ref, gt_ref, o_ref = refs[:-2 * hgs]
    else:
        u_ref, w_ref, qg_ref, kd_ref, att_ref, gt_ref, of_ref, z_ref, ng_ref, o_ref = refs[:-2 * hgs]
    hg = pl.program_id(1)
    step = pl.program_id(2)

    @pl.when(step == 0)
    def _():
        for s_ref in s_refs:
            s_ref[...] = jnp.zeros_like(s_ref)

    for vn_ref in vn_refs:
        vn_ref[...] = jnp.zeros_like(vn_ref)
    gtb = gt_ref[...]
    egs = [jnp.exp(_lane_col(gtb, direction * nh + hg * hgs + hh)) for hh in range(hgs)]
    n_chunks = LB // CHUNK
    for ci in range(n_chunks):
        cc = ci if direction == 0 else n_chunks - 1 - ci
        rs = slice(cc * CHUNK, (cc + 1) * CHUNK)
        sls = [slice(hh * hd, (hh + 1) * hd) for hh in range(hgs)]
        ss = [s_ref[...] for s_ref in s_refs]
        rr = [_bdot(jnp.concatenate([w_ref[rs, sl], qg_ref[rs, sl]], axis=0), s)
              for sl, s in zip(sls, ss)]
        for hh in range(hgs):
            v_new = u_ref[rs, sls[hh]] - rr[hh][:CHUNK]
            vn_refs[hh][rs, :] = v_new
            parts = [jnp.zeros((n, hd), F32) for n in (cc * CHUNK,) if n] + [v_new]
            parts += [jnp.zeros((n, hd), F32) for n in (LB - (cc + 1) * CHUNK,) if n]
            s_refs[hh][...] = (egs[hh][cc * CHUNK:cc * CHUNK + 1, :] * ss[hh]
                               + _bdot(kd_ref[sls[hh], :], jnp.concatenate(parts, axis=0)))
        for hh in range(hgs):
            sl = sls[hh]
            o = rr[hh][CHUNK:] + _bdot(att_ref[rs, hh * LB:(hh + 1) * LB], vn_refs[hh][...])
            if direction == 0:
                o_ref[rs, sl] = o
            else:
                o = o + of_ref[rs, sl]
                rms = lax.rsqrt(jnp.mean(o * o, axis=1, keepdims=True) + 1e-6)
                o_ref[rs, sl] = (o * rms * ng_ref[...] * _silu(z_ref[rs, sl])).astype(o_ref.dtype)


def gated_deltanet(order, qkv, p, gc, gr, gt, norm_g, nh, hd):
    t = qkv.shape[0]
    b = order.b
    w = nh * hd
    gw = DN_HG * hd
    ng = nh // DN_HG
    nchain = 2 * DN_HG

    per_dir_specs = ([pl.BlockSpec((LB, gw), lambda i, g: (i, g))] * 3 + [pl.BlockSpec((gw, LB), lambda i, g: (g, i))]
                     + [pl.BlockSpec((LB, DN_HG * LB), lambda i, g: (i, g))])
    per_dir_shape = ([jax.ShapeDtypeStruct((t, w), F32)] + [jax.ShapeDtypeStruct((t, w), BF16)] * 2
                     + [jax.ShapeDtypeStruct((w, t), BF16), jax.ShapeDtypeStruct((t, nh * LB), BF16)])
    ut = pl.pallas_call(
        functools.partial(_dn_ut_kernel, nh, hd),
        grid=(t // LB, ng),
        in_specs=[pl.BlockSpec((LB, gw), lambda i, g: (i, g)), pl.BlockSpec((LB, gw), lambda i, g: (i, ng + g)),
                  pl.BlockSpec((LB, gw), lambda i, g: (i, 2 * ng + g)),
                  pl.BlockSpec((LB, LANES), lambda i, g: (i, 0)),
                  pl.BlockSpec((None, LANES, LB), lambda i, g: (i, 0, 0)),
                  pl.BlockSpec((LB, LANES), lambda i, g: (i, 0))],
        out_specs=per_dir_specs * 2,
        out_shape=per_dir_shape * 2,
        scratch_shapes=[pltpu.VMEM((nchain, LB, LB), F32), pltpu.VMEM((nchain, LB, LB), F32),
                        pltpu.VMEM((nchain, LB, LB), BF16), pltpu.VMEM((nchain, LB, 2 * hd), BF16)],
        compiler_params=_cparams(("parallel", "parallel")),
        name="dn_ut",
    )(qkv, qkv, qkv, gc, gr, gt)
    outs = (ut[0:5], ut[5:10])

    hgs = min(DN_SCAN_HG, nh)
    sw = hgs * hd
    nsg = nh // hgs

    def run(direction, extra_in, extra_specs, out_dtype):
        def blk(width):
            return pl.BlockSpec((LB, width), lambda bi, g, s: (order.row_block(direction, bi, s), g))
        return pl.pallas_call(
            functools.partial(_dn_scan_kernel, direction, nh, hd, hgs),
            grid=(b, nsg, order.steps),
            in_specs=[blk(sw)] * 3 + [pl.BlockSpec((sw, LB), lambda bi, g, s: (g, order.row_block(direction, bi, s))),
                      blk(hgs * LB),
                      pl.BlockSpec((LB, LANES), lambda bi, g, s: (order.row_block(direction, bi, s), 0))] + extra_specs,
            out_specs=blk(sw),
            out_shape=jax.ShapeDtypeStruct((t, w), out_dtype),
            scratch_shapes=[pltpu.VMEM((hd, hd), F32)] * hgs + [pltpu.VMEM((LB, hd), F32)] * hgs,
            compiler_params=_cparams(("parallel", "parallel", "arbitrary")),
            name=f"dn_scan_d{direction}",
        )(*outs[direction], gt, *extra_in)

    o_fwd = run(0, [], [], F32)
    return run(1, [o_fwd, p, norm_g.reshape(1, hd)],
               [pl.BlockSpec((LB, sw), lambda bi, g, s: (order.row_block(1, bi, s), g)),
                pl.BlockSpec((LB, sw), lambda bi, g, s: (order.row_block(1, bi, s), 3 * nsg + g)),
                pl.BlockSpec((1, hd), lambda bi, g, s: (0, 0))], BF16)


def _pack_pairs(x):
    half = x.shape[1] // 2
    lo = pltpu.bitcast(x[:, :half].astype(BF16).astype(F32), U32) >> 16
    hi = pltpu.bitcast(x[:, half:].astype(BF16).astype(F32), U32) & jnp.uint32(0xFFFF0000)
    return hi | lo


def _unpack_pairs(w):
    lo = pltpu.bitcast(w << 16, F32)
    hi = pltpu.bitcast(w & jnp.uint32(0xFFFF0000), F32)
    return jnp.concatenate([lo, hi], axis=1)


def _ln_pack_kernel(alpha, n_ctx_tiles, *refs):
    if n_ctx_tiles is None:
        h_ref, z_ref, gate_ref, lng_ref, lnb_ref, sc_ref, sh_ref, hn_ref, vp_ref = refs
        h = h_ref[...]
    else:
        c_ref, x_ref, z_ref, gate_ref, lng_ref, lnb_ref, sc_ref, sh_ref, hn_ref, vp_ref = refs
        h = jnp.where(pl.program_id(0) < n_ctx_tiles, c_ref[...], x_ref[...])
    y = _layer_norm_rows(alpha * h + gate_ref[...] * z_ref[...], lng_ref[...], lnb_ref[...])
    hn_ref[...] = y
    vp_ref[...] = _pack_pairs(y * (1.0 + sc_ref[...]) + sh_ref[...])


def ln_residual_pack(rows, h, z, modv, layer, lng, lnb, alpha):
    tm = rows.tm // 2
    d = rows.d
    vec = pl.BlockSpec((1, d), lambda i: (0, 0))
    two_src = isinstance(h, tuple)
    h_specs = rows.src_specs(tm) if two_src else [rows.row_spec(tm)]
    h_args = list(h) if two_src else [h]
    return pl.pallas_call(
        functools.partial(_ln_pack_kernel, alpha, rows.n_ctx // tm if two_src else None),
        grid=(rows.t // tm,),
        in_specs=h_specs + [rows.row_spec(tm), rows.mod_spec(layer, 2, tm), vec, vec,
                            rows.mod_spec(layer, 4, tm), rows.mod_spec(layer, 3, tm)],
        out_specs=[rows.row_spec(tm), rows.row_spec(tm, d // 2)],
        out_shape=[jax.ShapeDtypeStruct((rows.t, d), F32), jax.ShapeDtypeStruct((rows.t, d // 2), U32)],
        compiler_params=_cparams(("parallel",)),
        name="ln_residual_pack",
    )(*h_args, z, modv, lng.reshape(1, d), lnb.reshape(1, d), modv, modv)


def _router_kernel(tm, ne, h_ref, sc_ref, sh_ref, w_ref, bias_ref, eid_ref, pos_ref, wgt_ref, cnt_ref, run_ref):
    i = pl.program_id(0)

    @pl.when(i == 0)
    def _():
        run_ref[...] = jnp.zeros_like(run_ref)

    x = h_ref[...] * (1.0 + sc_ref[...]) + sh_ref[...]
    w = w_ref[...]
    x_hi, w_hi = x.astype(BF16), w.astype(BF16)
    x_lo, w_lo = x - x_hi.astype(F32), w - w_hi.astype(F32)
    scores = _sigmoid(_bdot(x_hi, w_hi) + _bdot(x_hi, w_lo) + _bdot(x_lo, w_hi))
    biased = scores + bias_ref[...]
    lane = lax.broadcasted_iota(I32, (tm, ne), 1)
    gsz = ne // N_GROUPS
    grp = lane // gsz
    neg = -jnp.inf

    gs = []
    for g in range(N_GROUPS):
        m = jnp.where(grp == g, biased, neg)
        top1 = jnp.max(m, axis=1, keepdims=True)
        n_top = jnp.sum((m == top1).astype(F32), axis=1, keepdims=True)
        rest = jnp.max(jnp.where(m < top1, m, neg), axis=1, keepdims=True)
        gs.append(top1 + jnp.where(n_top >= 2.0, top1, rest))
    emask = jnp.zeros((tm, ne), jnp.bool_)
    for g in range(N_GROUPS):
        beaten = jnp.zeros((tm, 1), F32)
        for g2 in range(N_GROUPS):
            if g2 == g:
                continue
            wins = (gs[g2] > gs[g]) | (gs[g2] == gs[g]) if g2 < g else gs[g2] > gs[g]
            beaten = beaten + wins.astype(F32)
        emask = emask | ((grp == g) & (beaten < TOPK_GROUPS))
    masked = jnp.where(emask, biased, neg)

    sel = jnp.zeros((tm, ne), jnp.bool_)
    idxs = []
    lane_f = lane.astype(F32)
    for _ in range(TOP_K):
        mx = jnp.max(masked, axis=1, keepdims=True)
        idx = jnp.min(jnp.where(masked == mx, lane_f, float(ne)), axis=1, keepdims=True).astype(I32)
        hit = lane == idx
        sel = sel | hit
        masked = jnp.where(hit, neg, masked)
        idxs.append(idx)
    w_sel = jnp.where(sel, scores, 0.0)
    w_sel = w_sel / jnp.sum(w_sel, axis=1, keepdims=True) * ROUTED_SCALE

    ri = lax.broadcasted_iota(I32, (tm, tm), 0)
    ci = lax.broadcasted_iota(I32, (tm, tm), 1)
    self = sel.astype(F32)
    before = _bdot((ri > ci).astype(F32), self) + run_ref[0:1, :]
    run_ref[0:1, :] = run_ref[0:1, :] + jnp.sum(self, axis=0, keepdims=True)
    cnt_ref[...] = jnp.broadcast_to(run_ref[0:1, :], cnt_ref.shape)

    lane_o = lax.broadcasted_iota(I32, (tm, LANES), 1)
    eid = jnp.zeros((tm, LANES), I32)
    pos = jnp.zeros((tm, LANES), I32)
    wgt = jnp.zeros((tm, LANES), F32)
    for kk, idx in enumerate(idxs):
        hit = lane == idx
        p_k = jnp.sum(jnp.where(hit, before, 0.0), axis=1, keepdims=True).astype(I32)
        w_k = jnp.sum(jnp.where(hit, w_sel, 0.0), axis=1, keepdims=True)
        eid = jnp.where(lane_o == kk, idx, eid)
        pos = jnp.where(lane_o == kk, p_k, pos)
        wgt = jnp.where(lane_o == kk, w_k, wgt)
    eid_ref[...] = eid
    pos_ref[...] = pos
    wgt_ref[...] = wgt


def router(rows, h, modv, layer, router_w, router_bias):
    t, d = h.shape
    ne = router_w.shape[1]
    tm = rows.tm // 2
    row = pl.BlockSpec((tm, LANES), lambda i: (i, 0))
    return pl.pallas_call(
        functools.partial(_router_kernel, tm, ne),
        grid=(t // tm,),
        in_specs=[rows.row_spec(tm), rows.mod_spec(layer, 4, tm), rows.mod_spec(layer, 3, tm),
                  pl.BlockSpec((d, ne), lambda i: (0, 0)), pl.BlockSpec((1, ne), lambda i: (0, 0))],
        out_specs=[row, row, row, pl.BlockSpec((8, ne), lambda i: (0, 0))],
        out_shape=[jax.ShapeDtypeStruct((t, LANES), I32), jax.ShapeDtypeStruct((t, LANES), I32),
                   jax.ShapeDtypeStruct((t, LANES), F32), jax.ShapeDtypeStruct((8, ne), F32)],
        scratch_shapes=[pltpu.VMEM((8, ne), F32)],
        compiler_params=_cparams(("arbitrary",)),
        name="router",
    )(h, modv, modv, router_w, router_bias.reshape(1, ne))


def _dispatch_kernel(tm, ne, n_all_tiles, pstart_ref, npad_ref, nt_ref, dest_ref, x_ref, xs_ref, zero_ref, sem, sem_pad):
    def pad_copy(e, r):
        return pltpu.make_async_copy(zero_ref.at[pl.ds(0, 1), :], xs_ref.at[pl.ds(pstart_ref[e] + r, 1), :], sem_pad)

    def tail_copy(j):
        return pltpu.make_async_copy(zero_ref, xs_ref.at[pl.ds(pl.multiple_of(j * TME, TME), TME), :], sem_pad)

    @pl.when(pl.program_id(0) == 0)
    def _():
        zero_ref[...] = jnp.zeros_like(zero_ref)

        def fill(e, c):
            lax.fori_loop(0, npad_ref[e], lambda r, c2: (pad_copy(e, r).start(), c2)[1], 0)
            return c

        lax.fori_loop(0, ne, fill, 0)
        lax.fori_loop(nt_ref[0], n_all_tiles, lambda j, c: (tail_copy(j).start(), c)[1], 0)

        def drain(e, c):
            lax.fori_loop(0, npad_ref[e], lambda r, c2: (pad_copy(e, r).wait(), c2)[1], 0)
            return c

        lax.fori_loop(0, ne, drain, 0)
        lax.fori_loop(nt_ref[0], n_all_tiles, lambda j, c: (tail_copy(j).wait(), c)[1], 0)

    def issue(t, c):
        for kk in range(TOP_K):
            pltpu.make_async_copy(x_ref.at[pl.ds(t, 1), :], xs_ref.at[pl.ds(dest_ref[0, t * TOP_K + kk], 1), :], sem).start()
        return c

    lax.fori_loop(0, tm, issue, 0)
    for _ in range(TOP_K):
        pltpu.make_async_copy(x_ref, x_ref, sem).wait()


def dispatch(vp, dest, pad_start, n_pad, n_tiles, n_rows):
    t, dh = vp.shape
    ne = pad_start.shape[0]
    tm = _pick_tile(t, (256, 128))
    gs = pltpu.PrefetchScalarGridSpec(
        num_scalar_prefetch=3,
        grid=(t // tm,),
        in_specs=[pl.BlockSpec((None, 1, tm * TOP_K), lambda i, ps, npd, nt: (i, 0, 0), memory_space=pltpu.SMEM),
                  pl.BlockSpec((tm, dh), lambda i, ps, npd, nt: (i, 0))],
        out_specs=pl.BlockSpec(memory_space=pl.ANY),
        scratch_shapes=[pltpu.VMEM((TME, dh), U32), pltpu.SemaphoreType.DMA, pltpu.SemaphoreType.DMA],
    )
    return pl.pallas_call(
        functools.partial(_dispatch_kernel, tm, ne, n_rows // TME),
        grid_spec=gs,
        out_shape=jax.ShapeDtypeStruct((n_rows, dh), U32),
        compiler_params=_cparams(("arbitrary",)),
        name="moe_dispatch",
    )(pad_start, n_pad, n_tiles, dest.reshape(t // tm, 1, tm * TOP_K), vp)


def _swiglu_kernel(eh, pack_out, te_ref, nt_ref, x_ref, wi_ref, wo_ref, o_ref, wib_ref, wob_ref):
    j = pl.program_id(0)

    @pl.when(j < nt_ref[0])
    def _():
        @pl.when((j == 0) | (te_ref[j] != te_ref[jnp.maximum(j - 1, 0)]))
        def _():
            wib_ref[...] = wi_ref[...].astype(wib_ref.dtype)
            wob_ref[...] = wo_ref[...].astype(wob_ref.dtype)

        ab = _bdot(_unpack_pairs(x_ref[...]), wib_ref[...])
        act = _silu(ab[:, :eh]) * ab[:, eh:]
        y = _bdot(act, wob_ref[...])
        o_ref[...] = _pack_pairs(y) if pack_out else y

    @pl.when(j >= nt_ref[0])
    def _():
        o_ref[...] = jnp.zeros_like(o_ref)


def grouped_swiglu(xs, w_in, w_out, layer, tile_expert, n_tiles, tile, pack_out):
    r, dh = xs.shape
    _, ne, d, eh2 = w_in.shape
    eh = eh2 // 2
    grid_n = r // tile
    out_w, out_dtype = (dh, U32) if pack_out else (d, F32)

    def row_map(j, te, nt):
        return (jnp.minimum(j, nt[0] - 1), 0)

    gs = pltpu.PrefetchScalarGridSpec(
        num_scalar_prefetch=2,
        grid=(grid_n,),
        in_specs=[pl.BlockSpec((tile, dh), row_map),
                  pl.BlockSpec((None, None, d, eh2), lambda j, te, nt: (layer, te[j], 0, 0)),
                  pl.BlockSpec((None, None, eh, d), lambda j, te, nt: (layer, te[j], 0, 0))],
        out_specs=pl.BlockSpec((tile, out_w), lambda j, te, nt: (j, 0)),
        scratch_shapes=[pltpu.VMEM((d, eh2), BF16), pltpu.VMEM((eh, d), BF16)],
    )
    return pl.pallas_call(
        functools.partial(_swiglu_kernel, eh, pack_out),
        grid_spec=gs,
        out_shape=jax.ShapeDtypeStruct((r, out_w), out_dtype),
        compiler_params=_cparams(("arbitrary",)),
        name="grouped_swiglu",
    )(tile_expert, n_tiles, xs, w_in, w_out)


def _combine_kernel(tm, alpha, has_next, dest_ref, destn_ref, wgt_ref, sh_ref, h_ref, gate_ref, lng_ref, lnb_ref,
                    sc_ref, sh2_ref, ys_ref, *rest):
    if has_next:
        hn_ref, u_ref, buf_ref, sem = rest
    else:
        hn_ref, buf_ref, sem = rest
    i = pl.program_id(0)
    slot = i % 2

    def gather(d_ref, s):
        def body(t, c):
            for kk in range(TOP_K):
                pltpu.make_async_copy(ys_ref.at[pl.ds(d_ref[0, t * TOP_K + kk], 1), :],
                                      buf_ref.at[s, kk, pl.ds(t, 1), :], sem.at[s]).start()
            return c
        lax.fori_loop(0, tm, body, 0)

    @pl.when(i == 0)
    def _():
        gather(dest_ref, 0)

    @pl.when(i + 1 < pl.num_programs(0))
    def _():
        gather(destn_ref, 1 - slot)

    pltpu.make_async_copy(buf_ref.at[slot], buf_ref.at[slot], sem.at[slot]).wait()

    wgt = wgt_ref[...]
    f = sh_ref[...]
    for kk in range(TOP_K):
        f = f + _lane_col(wgt, kk) * _unpack_pairs(buf_ref[slot, kk])
    y = _layer_norm_rows(alpha * h_ref[...] + gate_ref[...] * f, lng_ref[...], lnb_ref[...])
    hn_ref[...] = y
    if has_next:
        u_ref[...] = (y * (1.0 + sc_ref[...]) + sh2_ref[...]).astype(u_ref.dtype)


def moe_combine_ln(rows, ys, dest, wgt, shared, h, modv, layer, lng, lnb, alpha, has_next):
    tm = 128
    t, d = rows.t, rows.d
    dh = ys.shape[1]
    n_all = t // tm
    off = 0 if has_next else rows.n_ctx // tm
    n_out = n_all - off
    vec = pl.BlockSpec((1, d), lambda i: (0, 0))
    nl = layer + 1 if has_next else layer

    def row(width):
        return pl.BlockSpec((tm, width), lambda i: (i + off, 0))

    def mod(lyr, which):
        return pl.BlockSpec((None, 1, d), lambda i: ((lyr * MOD_ROWS + rows.seg(i + off, tm)) * 6 + which, 0, 0))

    out_specs = [pl.BlockSpec((tm, d), lambda i: (i, 0))]
    out_shape = [jax.ShapeDtypeStruct((n_out * tm, d), F32)]
    if has_next:
        out_specs.append(pl.BlockSpec((tm, d), lambda i: (i, 0)))
        out_shape.append(jax.ShapeDtypeStruct((n_out * tm, d), BF16))
    dest3 = dest.reshape(n_all, 1, tm * TOP_K)
    return pl.pallas_call(
        functools.partial(_combine_kernel, tm, alpha, has_next),
        grid=(n_out,),
        in_specs=[pl.BlockSpec((None, 1, tm * TOP_K), lambda i: (i + off, 0, 0), memory_space=pltpu.SMEM),
                  pl.BlockSpec((None, 1, tm * TOP_K), lambda i: (jnp.minimum(i + off + 1, n_all - 1), 0, 0),
                               memory_space=pltpu.SMEM),
                  row(LANES), row(d), row(d), mod(layer, 5), vec, vec, mod(nl, 1), mod(nl, 0),
                  pl.BlockSpec(memory_space=pl.ANY)],
        out_specs=out_specs,
        out_shape=out_shape,
        scratch_shapes=[pltpu.VMEM((2, TOP_K, tm, dh), U32), pltpu.SemaphoreType.DMA((2,))],
        compiler_params=_cparams(("arbitrary",)),
        name="moe_combine_ln",
    )(dest3, dest3, wgt, shared, h, modv, lng.reshape(1, d), lnb.reshape(1, d), modv, modv, ys)


def moe_ffn(rows, vp, h, modv, layer, lng, lnb, alpha, has_next, router_w, router_bias, exp_w_in, exp_w_out,
            sh_w_in, sh_w_out):
    t = vp.shape[0]
    ne = router_w.shape[1]
    eid, pos, wgt, cnt = router(rows, h, modv, layer, router_w, router_bias)
    counts = cnt[0].astype(I32)
    padded = (counts + TME - 1) // TME * TME
    ends = jnp.cumsum(padded)
    offs = ends - padded
    dest = offs[eid[:, :TOP_K]] + pos[:, :TOP_K]
    n_rows = (t * TOP_K // TME + ne) * TME
    tile_start = jnp.arange(n_rows // TME, dtype=I32) * TME
    tile_expert = jnp.minimum(jnp.sum((tile_start[:, None] >= ends[None, :]).astype(I32), axis=1), ne - 1)
    n_tiles = (ends[-1] // TME).reshape(1)

    xs = dispatch(vp, dest, offs + counts, padded - counts, n_tiles, n_rows)
    ys = grouped_swiglu(xs, exp_w_in, exp_w_out, layer, tile_expert, n_tiles, TME, True)
    st = rows.tm // 2
    shared = grouped_swiglu(vp, sh_w_in[:, None], sh_w_out[:, None], layer, jnp.zeros((t // st,), I32),
                            jnp.full((1,), t // st, I32), st, False)
    return moe_combine_ln(rows, ys, dest, wgt, shared, h, modv, layer, lng, lnb, alpha, has_next)


def _pad_cols(w, n):
    return jnp.pad(w, ((0, 0), (0, n - w.shape[1])))


def _pad_row(v, n=LANES):
    v = v.reshape(1, -1).astype(F32)
    return jnp.pad(v, ((0, 0), (0, n - v.shape[1])))


def retention_mlstm_mixer(rows, order, u, w_in, w_out, ret_decay_rate, ret_norm_g, ml_gate_b, ml_norm_g):
    nh_r = ret_decay_rate.shape[-1]
    nh_m = ml_gate_b.shape[-1]
    ret_w = ret_norm_g.shape[0]
    ml_w = ml_norm_g.shape[0]
    hd_r, hd_m = ret_w // nh_r, ml_w // nh_m
    assert hd_r == hd_m
    main_w = 4 * ret_w + 4 * ml_w
    wb = w_in.astype(BF16)
    p = matmul(u, wb[:, :main_w], name="ab_in_proj")
    pg = matmul(u, _pad_cols(wb[:, main_w:], LANES), name="ab_gate_proj")

    pos = jnp.arange(rows.ctx + rows.seq, dtype=F32)
    half = hd_r // 2
    inv_freq = ROPE_BASE ** (-jnp.arange(half, dtype=F32) / half)
    ang = pos[:, None] * inv_freq[None, :]
    rate = jnp.broadcast_to(ret_decay_rate.astype(F32).reshape(2 * nh_r, 1, 1), (2 * nh_r, 1, LANES))
    y_r = retention(order, p, jnp.cos(ang), jnp.sin(ang), rate, ret_norm_g, nh_r, hd_r)

    gc, gr = ml_gates(pg, _pad_row(ml_gate_b), nh_m)
    y_m = mlstm(order, p, 4 * nh_r, gc, gr, ml_norm_g, nh_m, hd_m)
    return matmul2(y_r, y_m, w_out.astype(BF16), name="ab_out_proj")


def gated_deltanet_mixer(rows, order, u, w_in, conv_w, a_log, dt_bias, norm_g, w_out):
    nh = a_log.shape[-1]
    hd = norm_g.shape[0]
    dn_w = nh * hd
    wb = w_in.astype(BF16)
    p = matmul(u, wb[:, :4 * dn_w], name="dn_in_proj")
    pg = matmul(u, _pad_cols(wb[:, 4 * dn_w:], LANES), name="dn_gate_proj")
    conv_w9 = jnp.pad(conv_w.reshape(9, 3 * dn_w).astype(F32), ((0, 7), (0, 0)))
    qkv = dn_conv(rows, p, conv_w9, dn_w, hd)
    gc, gr, gt = dn_gates(pg, _pad_row(a_log), _pad_row(dt_bias), nh)
    y = gated_deltanet(order, qkv, p, gc, gr, gt, norm_g, nh, hd)
    return matmul(y, w_out.astype(BF16), name="dn_out_proj")


def kernel(x, c, ctx, c_ctx, ada_w, ada_b, ln_g, ln_b, ab_w_in, ab_w_out, ret_decay_rate, ret_norm_g, ml_gate_b, ml_norm_g, dn_w_in, dn_conv_w, dn_a_log, dn_dt_bias, dn_norm_g, dn_w_out, router_w, router_bias, exp_w_in, exp_w_out, sh_w_in, sh_w_out):
    b, seq, d = x.shape
    n_ctx = ctx.shape[1]
    depth = ada_w.shape[0]
    alpha = (2 * depth) ** 0.25
    rows = Rows(b, n_ctx, seq, d)
    order = ScanOrder(b, n_ctx, seq)
    assert b + 1 <= MOD_ROWS

    cvec = jnp.zeros((MOD_ROWS, d), F32).at[:b].set(c).at[b].set(c_ctx)
    modv = ada_table(cvec, ada_w, ada_b).reshape(depth * MOD_ROWS * 6, 1, d)

    h = (ctx.reshape(b * n_ctx, d), x.reshape(b * seq, d))
    u = modulate(rows, h[0], h[1], modv, 0, 1, 0)
    for layer in range(depth):
        j = layer // 2
        if layer % 2 == 0:
            z = retention_mlstm_mixer(rows, order, u, ab_w_in[j], ab_w_out[j], ret_decay_rate[j], ret_norm_g[j],
                                      ml_gate_b[j], ml_norm_g[j])
        else:
            z = gated_deltanet_mixer(rows, order, u, dn_w_in[j], dn_conv_w[j], dn_a_log[j], dn_dt_bias[j],
                                     dn_norm_g[j], dn_w_out[j])
        h, vp = ln_residual_pack(rows, h, z, modv, layer, ln_g[layer, 0], ln_b[layer, 0], alpha)
        has_next = layer < depth - 1
        outs = moe_ffn(rows, vp, h, modv, layer, ln_g[layer, 1], ln_b[layer, 1], alpha, has_next, router_w[layer],
                       router_bias[layer], exp_w_in, exp_w_out, sh_w_in, sh_w_out)
        if has_next:
            h, u = outs
        else:
            h = outs[0]
    return h.reshape(b, seq, d)
```

```python
import functools
import math

import jax
import jax.numpy as jnp
from jax import lax
from jax.experimental import pallas as pl
from jax.experimental.pallas import tpu as pltpu

F32 = jnp.float32
BF16 = jnp.bfloat16
I32 = jnp.int32
U32 = jnp.uint32

V7X_VMEM_BYTES = 64 * 1024 * 1024
VMEM_LIMIT = V7X_VMEM_BYTES - 8 * 1024 * 1024
LANES = 128

GRID_W = 64
CHUNK = 64
TOP_K = 8
N_GROUPS = 8
TOPK_GROUPS = 4
ROUTED_SCALE = 2.5
ROPE_BASE = 10000.0
LN_EPS = 1e-5

LB = 256
SCAN_HG = 4
TME = 512
MOD_ROWS = 8


def _cparams(sem):
    return pltpu.CompilerParams(dimension_semantics=sem, vmem_limit_bytes=VMEM_LIMIT)


def _bdot(a, b):
    return jnp.dot(a.astype(BF16), b.astype(BF16), preferred_element_type=F32)


def _bdot_nt(a, b):
    return lax.dot_general(a.astype(BF16), b.astype(BF16), (((1,), (1,)), ((), ())), preferred_element_type=F32)


def _bdot_tn(a, b):
    return lax.dot_general(a.astype(BF16), b.astype(BF16), (((0,), (0,)), ((), ())), preferred_element_type=F32)


def _fdot(a, b):
    return jnp.dot(a, b, preferred_element_type=F32, precision=lax.Precision.HIGHEST)


def _sigmoid(x):
    return 1.0 / (1.0 + jnp.exp(-x))


def _silu(x):
    return x * _sigmoid(x)


def _softplus(x):
    return jnp.maximum(x, 0.0) + jnp.log(1.0 + jnp.exp(-jnp.abs(x)))


def _log_sigmoid(x):
    return -_softplus(-x)


def _lane_col(x, lane):
    idx = lax.broadcasted_iota(I32, x.shape, 1)
    return jnp.sum(jnp.where(idx == lane, x, 0.0), axis=1, keepdims=True)


def _ada_kernel(c_ref, w_ref, b_ref, o_ref):
    o_ref[...] = _bdot(_silu(c_ref[...]), w_ref[...]) + b_ref[...]


def ada_table(cvec, ada_w, ada_b):
    depth, d, n = ada_w.shape
    tn = 512
    return pl.pallas_call(
        _ada_kernel,
        grid=(depth, n // tn),
        in_specs=[pl.BlockSpec((MOD_ROWS, d), lambda l, j: (0, 0)),
                  pl.BlockSpec((None, d, tn), lambda l, j: (l, 0, j)),
                  pl.BlockSpec((None, 1, tn), lambda l, j: (l, 0, j))],
        out_specs=pl.BlockSpec((None, MOD_ROWS, tn), lambda l, j: (l, 0, j)),
        out_shape=jax.ShapeDtypeStruct((depth, MOD_ROWS, n), F32),
        compiler_params=_cparams(("parallel", "parallel")),
        name="ada_table",
    )(cvec, ada_w, ada_b.reshape(depth, 1, n))


class Rows:
    def __init__(self, b, ctx, seq, d):
        self.b, self.ctx, self.seq, self.d = b, ctx, seq, d
        self.n_ctx = b * ctx
        self.t = b * ctx + b * seq
        self.tm = math.gcd(math.gcd(self.n_ctx, seq), 512)

    def seg(self, i, tm):
        n_ctx_tiles = self.n_ctx // tm
        return jnp.where(i < n_ctx_tiles, self.b, (i - n_ctx_tiles) // (self.seq // tm))

    def mod_spec(self, layer, which, tm):
        return pl.BlockSpec((None, 1, self.d), lambda i: ((layer * MOD_ROWS + self.seg(i, tm)) * 6 + which, 0, 0))

    def row_spec(self, tm, width=None):
        return pl.BlockSpec((tm, self.d if width is None else width), lambda i: (i, 0))

    def src_specs(self, tm):
        nct = self.n_ctx // tm
        return [pl.BlockSpec((tm, self.d), lambda i: (jnp.minimum(i, nct - 1), 0)),
                pl.BlockSpec((tm, self.d), lambda i: (jnp.maximum(i - nct, 0), 0))]


def _mod_kernel(n_ctx_tiles, c_ref, x_ref, sc_ref, sh_ref, o_ref):
    h = jnp.where(pl.program_id(0) < n_ctx_tiles, c_ref[...], x_ref[...])
    o_ref[...] = (h * (1.0 + sc_ref[...]) + sh_ref[...]).astype(o_ref.dtype)


def modulate(rows, ctx2d, x2d, modv, layer, sc_idx, sh_idx):
    tm = rows.tm
    return pl.pallas_call(
        functools.partial(_mod_kernel, rows.n_ctx // tm),
        grid=(rows.t // tm,),
        in_specs=rows.src_specs(tm) + [rows.mod_spec(layer, sc_idx, tm), rows.mod_spec(layer, sh_idx, tm)],
        out_specs=rows.row_spec(tm),
        out_shape=jax.ShapeDtypeStruct((rows.t, rows.d), BF16),
        compiler_params=_cparams(("parallel",)),
        name="modulate",
    )(ctx2d, x2d, modv, modv)


def _layer_norm_rows(r, g, b):
    mu = jnp.mean(r, axis=-1, keepdims=True)
    xc = r - mu
    var = jnp.mean(xc * xc, axis=-1, keepdims=True)
    return xc * lax.rsqrt(var + LN_EPS) * g + b


def _mm_kernel(x_ref, w_ref, o_ref):
    o_ref[...] = _bdot(x_ref[...], w_ref[...]).astype(o_ref.dtype)


def _pick_tile(n, prefs):
    for p in prefs:
        if n % p == 0:
            return p
    return n


def matmul(x, w, out_dtype=F32, name="matmul"):
    m, k = x.shape
    n = w.shape[1]
    tm = _pick_tile(m, (768, 512, 256, 128))
    tn = _pick_tile(n, (1024, 512, 256, 128))
    return pl.pallas_call(
        _mm_kernel,
        grid=(n // tn, m // tm),
        in_specs=[pl.BlockSpec((tm, k), lambda j, i: (i, 0)), pl.BlockSpec((k, tn), lambda j, i: (0, j))],
        out_specs=pl.BlockSpec((tm, tn), lambda j, i: (i, j)),
        out_shape=jax.ShapeDtypeStruct((m, n), out_dtype),
        compiler_params=_cparams(("parallel", "parallel")),
        name=name,
    )(x, w)


def _mm2_kernel(x1_ref, x2_ref, w1_ref, w2_ref, o_ref):
    o_ref[...] = _bdot(x1_ref[...], w1_ref[...]) + _bdot(x2_ref[...], w2_ref[...])


def matmul2(x1, x2, w, name="matmul2"):
    m, k1 = x1.shape
    k2 = x2.shape[1]
    n = w.shape[1]
    tm = _pick_tile(m, (768, 512, 256, 128))
    tn = _pick_tile(n, (1024, 512, 256, 128))
    return pl.pallas_call(
        _mm2_kernel,
        grid=(n // tn, m // tm),
        in_specs=[pl.BlockSpec((tm, k1), lambda j, i: (i, 0)), pl.BlockSpec((tm, k2), lambda j, i: (i, 0)),
                  pl.BlockSpec((k1, tn), lambda j, i: (0, j)), pl.BlockSpec((k2, tn), lambda j, i: (k1 // k2, j))],
        out_specs=pl.BlockSpec((tm, tn), lambda j, i: (i, j)),
        out_shape=jax.ShapeDtypeStruct((m, n), F32),
        compiler_params=_cparams(("parallel", "parallel")),
        name=name,
    )(x1, x2, w, w)


class ScanOrder:
    def __init__(self, b, ctx, seq):
        assert ctx % LB == 0 and seq % LB == 0
        self.b = b
        self.nc = ctx // LB
        self.nl = seq // LB
        self.steps = self.nc + self.nl

    def row_block(self, direction, bi, s):
        nc, nl = self.nc, self.nl
        if direction == 0:
            c, l = s, s - nc
        else:
            c, l = nc - 1 - s, nl - 1 - (s - nc)
        return jnp.where(s < nc, bi * nc + c, self.b * nc + bi * nl + l)

    def pos_block(self, direction, s):
        nc, nl = self.nc, self.nl
        if direction == 0:
            return s
        return jnp.where(s < nc, nc - 1 - s, nc + nl - 1 - (s - nc))


def _causal(direction, n):
    ri = lax.broadcasted_iota(I32, (n, n), 0)
    ci = lax.broadcasted_iota(I32, (n, n), 1)
    rel = ri - ci if direction == 0 else ci - ri
    return rel >= 0, rel > 0, rel


def _rope(t, cos, sin):
    half = t.shape[-1] // 2
    t1, t2 = t[:, :half], t[:, half:]
    return jnp.concatenate([t1 * cos - t2 * sin, t1 * sin + t2 * cos], axis=-1)


def _head_ln_gate(o, g, gate_act):
    mu = jnp.mean(o, axis=-1, keepdims=True)
    xc = o - mu
    var = jnp.mean(xc * xc, axis=-1, keepdims=True)
    return xc * lax.rsqrt(var + LN_EPS) * g * gate_act


def _ret_kernel(direction, hd, *refs):
    ng = SCAN_HG
    s_refs = refs[-ng:]
    if direction == 0:
        q_ref, k_ref, v_ref, cos_ref, sin_ref, rate_ref, o_ref = refs[:-ng]
    else:
        q_ref, k_ref, v_ref, cos_ref, sin_ref, rate_ref, of_ref, g_ref, ng_ref, o_ref = refs[:-ng]
    step = pl.program_id(2)

    @pl.when(step == 0)
    def _():
        for s_ref in s_refs:
            s_ref[...] = jnp.zeros_like(s_ref)

    incl, _, rel = _causal(direction, LB)
    relf = jnp.maximum(rel, 0).astype(F32)
    ri = lax.broadcasted_iota(I32, (LB, 1), 0).astype(F32)
    q_pow, k_pow = (ri + 1.0, LB - 1.0 - ri) if direction == 0 else (LB - ri, ri)
    cos, sin = cos_ref[...], sin_ref[...]
    heads = range(ng)
    sls = [slice(hh * hd, (hh + 1) * hd) for hh in heads]
    lgs = [-jnp.exp(rate_ref[hh])[:, :1] for hh in heads]
    qs = [_rope(q_ref[:, sl], cos, sin) for sl in sls]
    ks = [_rope(k_ref[:, sl], cos, sin) * hd ** -0.5 for sl in sls]
    vs = [v_ref[:, sl] for sl in sls]
    ss = [s_ref[...] for s_ref in s_refs]
    atts = [_bdot_nt(q, k) * jnp.where(incl, jnp.exp(lg * relf), 0.0) for q, k, lg in zip(qs, ks, lgs)]
    inter = [_bdot(q * jnp.exp(lg * q_pow), s) for q, lg, s in zip(qs, lgs, ss)]
    outs = [_bdot(att, v) + oi for att, v, oi in zip(atts, vs, inter)]
    for hh in heads:
        lg = lgs[hh]
        s_refs[hh][...] = ss[hh] * jnp.exp(lg * LB) + _bdot_tn(ks[hh] * jnp.exp(lg * k_pow), vs[hh])
    for hh in heads:
        sl = sls[hh]
        if direction == 0:
            o_ref[:, sl] = outs[hh]
        else:
            o_ref[:, sl] = _head_ln_gate(outs[hh] + of_ref[:, sl], ng_ref[:, sl], _silu(g_ref[:, sl])).astype(o_ref.dtype)


def retention(order, p, cos, sin, rate, norm_g, nh, hd):
    t = p.shape[0]
    b = order.b
    half = hd // 2
    g = SCAN_HG
    assert nh % g == 0
    gw = g * hd

    def run(direction, extra_in, extra_specs, out_dtype):
        def blk(col0):
            return pl.BlockSpec((LB, gw), lambda bi, h, s: (order.row_block(direction, bi, s), col0 // g + h))
        in_specs = [blk(0), blk(nh), blk(2 * nh),
                    pl.BlockSpec((LB, half), lambda bi, h, s: (order.pos_block(direction, s), 0)),
                    pl.BlockSpec((LB, half), lambda bi, h, s: (order.pos_block(direction, s), 0)),
                    pl.BlockSpec((g, 1, LANES), lambda bi, h, s: (direction * nh // g + h, 0, 0))] + extra_specs
        return pl.pallas_call(
            functools.partial(_ret_kernel, direction, hd),
            grid=(b, nh // g, order.steps),
            in_specs=in_specs,
            out_specs=pl.BlockSpec((LB, gw), lambda bi, h, s: (order.row_block(direction, bi, s), h)),
            out_shape=jax.ShapeDtypeStruct((t, nh * hd), out_dtype),
            scratch_shapes=[pltpu.VMEM((hd, hd), F32)] * g,
            compiler_params=_cparams(("parallel", "parallel", "arbitrary")),
            name=f"retention_d{direction}",
        )(p, p, p, cos, sin, rate, *extra_in)

    o_fwd = run(0, [], [], F32)
    return run(1, [o_fwd, p, norm_g.reshape(1, nh * hd)],
               [pl.BlockSpec((LB, gw), lambda bi, h, s: (order.row_block(1, bi, s), h)),
                pl.BlockSpec((LB, gw), lambda bi, h, s: (order.row_block(1, bi, s), 3 * nh // g + h)),
                pl.BlockSpec((1, gw), lambda bi, h, s: (0, h))], BF16)


def _ml_gate_kernel(nh, g_ref, b_ref, gc_ref, gr_ref):
    g = g_ref[...] + b_ref[...]
    ls = _log_sigmoid(g)
    incl_f, _, _ = _causal(0, LB)
    incl_b, _, _ = _causal(1, LB)
    pre = _fdot(incl_f.astype(F32), ls)
    suf = _fdot(incl_b.astype(F32), ls)
    lane = lax.broadcasted_iota(I32, g.shape, 1)
    is_f = (lane // nh) % 2 == 1
    is_d1 = lane // (2 * nh) == 1
    out = jnp.where(is_f, jnp.where(is_d1, suf, pre), g)
    gc_ref[...] = out
    gr_ref[...] = out.T


def ml_gates(pg, bias_row, nh):
    t = pg.shape[0]
    return pl.pallas_call(
        functools.partial(_ml_gate_kernel, nh),
        grid=(t // LB,),
        in_specs=[pl.BlockSpec((LB, LANES), lambda i: (i, 0)), pl.BlockSpec((1, LANES), lambda i: (0, 0))],
        out_specs=[pl.BlockSpec((LB, LANES), lambda i: (i, 0)), pl.BlockSpec((None, LANES, LB), lambda i: (i, 0, 0))],
        out_shape=[jax.ShapeDtypeStruct((t, LANES), F32), jax.ShapeDtypeStruct((t // LB, LANES, LB), F32)],
        compiler_params=_cparams(("parallel",)),
        name="ml_gates",
    )(pg, bias_row)


def _ml_kernel(direction, nh, hd, *refs):
    ng = SCAN_HG
    c_refs, n_refs, m_refs = refs[-3 * ng:-2 * ng], refs[-2 * ng:-ng], refs[-ng:]
    if direction == 0:
        q_ref, k_ref, v_ref, gc_ref, gr_ref, o_ref = refs[:-3 * ng]
    else:
        q_ref, k_ref, v_ref, gc_ref, gr_ref, of_ref, g_ref, ng_ref, o_ref = refs[:-3 * ng]
    grp = pl.program_id(1)
    step = pl.program_id(2)

    @pl.when(step == 0)
    def _():
        for ref in refs[-3 * ng:]:
            ref[...] = jnp.zeros_like(ref)

    gc = gc_ref[...]
    incl, _, _ = _causal(direction, LB)
    last = LB - 1 if direction == 0 else 0
    heads = range(ng)
    sls = [slice(hh * hd, (hh + 1) * hd) for hh in heads]
    lanes_i = [direction * 2 * nh + grp * ng + hh for hh in heads]
    ig_cols = [_lane_col(gc, li) for li in lanes_i]
    b_cols = [_lane_col(gc, li + nh) for li in lanes_i]
    d_logs = [jnp.where(incl, b_col - gr_ref[pl.ds(li + nh, 1), :] + gr_ref[pl.ds(li, 1), :], -jnp.inf)
              for b_col, li in zip(b_cols, lanes_i)]
    inter_logs = [b_col + m_ref[0:1, 0:1] for b_col, m_ref in zip(b_cols, m_refs)]
    m_rows = [jnp.maximum(il, jnp.max(dl, axis=1, keepdims=True)) for il, dl in zip(inter_logs, d_logs)]
    inter_ws = [jnp.exp(il - mr) for il, mr in zip(inter_logs, m_rows)]
    qs = [q_ref[:, sl] for sl in sls]
    ks = [k_ref[:, sl] * hd ** -0.5 for sl in sls]
    vs = [v_ref[:, sl] for sl in sls]
    cs = [c_ref[...] for c_ref in c_refs]
    ns = [n_ref[0:1, :] for n_ref in n_refs]
    ss = [_bdot_nt(q, k) * jnp.exp(dl - mr) for q, k, dl, mr in zip(qs, ks, d_logs, m_rows)]
    qcs = [_bdot(q, c) for q, c in zip(qs, cs)]
    nums = [_bdot(s, v) + iw * qc for s, v, iw, qc in zip(ss, vs, inter_ws, qcs)]
    houts = []
    for hh in heads:
        den = (jnp.sum(ss[hh], axis=1, keepdims=True)
               + inter_ws[hh] * jnp.sum(qs[hh] * ns[hh], axis=1, keepdims=True))
        houts.append(nums[hh] / jnp.maximum(jnp.abs(den), jnp.exp(-m_rows[hh])))
    for hh in heads:
        b_col, m_row = b_cols[hh], m_rows[hh]
        m_last = m_row[last:last + 1, :]
        dec_last = inter_ws[hh][last:last + 1, :]
        w_last = jnp.exp(b_col[last:last + 1, :] - b_col + ig_cols[hh] - m_last)
        kw = ks[hh] * w_last
        c_refs[hh][...] = dec_last * cs[hh] + _bdot_tn(kw, vs[hh])
        n_refs[hh][0:1, :] = dec_last * ns[hh] + jnp.sum(kw, axis=0, keepdims=True)
        m_refs[hh][...] = jnp.broadcast_to(m_last, m_refs[hh].shape)
    for hh in heads:
        sl = sls[hh]
        if direction == 0:
            o_ref[:, sl] = houts[hh]
        else:
            o_ref[:, sl] = _head_ln_gate(houts[hh] + of_ref[:, sl], ng_ref[:, sl],
                                         _sigmoid(g_ref[:, sl])).astype(o_ref.dtype)


def mlstm(order, p, col0, gc, gr, norm_g, nh, hd):
    t = p.shape[0]
    b = order.b
    g = SCAN_HG
    assert nh % g == 0 and col0 % g == 0
    gw = g * hd

    def run(direction, extra_in, extra_specs, out_dtype):
        def blk(c):
            return pl.BlockSpec((LB, gw), lambda bi, h, s: (order.row_block(direction, bi, s), (col0 + c) // g + h))
        in_specs = [blk(0), blk(nh), blk(2 * nh),
                    pl.BlockSpec((LB, LANES), lambda bi, h, s: (order.row_block(direction, bi, s), 0)),
                    pl.BlockSpec((None, LANES, LB), lambda bi, h, s: (order.row_block(direction, bi, s), 0, 0))]
        return pl.pallas_call(
            functools.partial(_ml_kernel, direction, nh, hd),
            grid=(b, nh // g, order.steps),
            in_specs=in_specs + extra_specs,
            out_specs=pl.BlockSpec((LB, gw), lambda bi, h, s: (order.row_block(direction, bi, s), h)),
            out_shape=jax.ShapeDtypeStruct((t, nh * hd), out_dtype),
            scratch_shapes=([pltpu.VMEM((hd, hd), F32)] * g + [pltpu.VMEM((8, hd), F32)] * g
                            + [pltpu.VMEM((8, LANES), F32)] * g),
            compiler_params=_cparams(("parallel", "parallel", "arbitrary")),
            name=f"mlstm_d{direction}",
        )(p, p, p, gc, gr, *extra_in)

    o_fwd = run(0, [], [], F32)
    return run(1, [o_fwd, p, norm_g.reshape(1, nh * hd)],
               [pl.BlockSpec((LB, gw), lambda bi, h, s: (order.row_block(1, bi, s), h)),
                pl.BlockSpec((LB, gw), lambda bi, h, s: (order.row_block(1, bi, s), (col0 + 3 * nh) // g + h)),
                pl.BlockSpec((1, gw), lambda bi, h, s: (0, h))], BF16)


CONV_HALO = 128


def _conv_kernel(tb, tc, n_ctx, ctx, seq, n_qk_tiles, n_q_tiles, hd, xp_ref, xc_ref, xn_ref, w_ref, o_ref, buf_ref, lr_ref):
    i = pl.program_id(0)
    j = pl.program_id(1)
    is_ctx = i * tb < n_ctx
    lat0 = (i * tb - n_ctx) % seq
    top = jnp.logical_not(is_ctx) & (lat0 == 0)
    bottom = jnp.logical_not(is_ctx) & (lat0 == seq - tb)
    buf_ref[0:CONV_HALO, :] = jnp.where(top, 0.0, xp_ref[...])
    buf_ref[CONV_HALO:CONV_HALO + tb, :] = xc_ref[...]
    buf_ref[CONV_HALO + tb:, :] = jnp.where(bottom, 0.0, xn_ref[...])

    ext = tb + 2 * GRID_W
    base = CONV_HALO - GRID_W
    r = lax.broadcasted_iota(I32, (ext, 1), 0)
    col = jnp.where(is_ctx, (r + (ctx - GRID_W % ctx)) % ctx, r % GRID_W)
    n_c = jnp.where(is_ctx, ctx, GRID_W)
    lr_ref[0] = jnp.where(col >= 1, buf_ref[base - 1:base - 1 + ext, :], 0.0)
    lr_ref[1] = jnp.where(col + 1 < n_c, buf_ref[base + 1:base + 1 + ext, :], 0.0)

    acc = None
    for dr in range(3):
        wl, wc, wr = (w_ref[dr * 3 + dc:dr * 3 + dc + 1, :] for dc in range(3))
        if dr != 1:
            wl, wc, wr = (jnp.where(is_ctx, 0.0, w) for w in (wl, wc, wr))
        o0 = CONV_HALO + (dr - 1) * GRID_W
        e0 = dr * GRID_W
        term = (lr_ref[0, e0:e0 + tb, :] * wl + buf_ref[o0:o0 + tb, :] * wc + lr_ref[1, e0:e0 + tb, :] * wr)
        acc = term if acc is None else acc + term
    y = _silu(acc)
    qscale = jnp.where(j < n_q_tiles, hd ** -0.5, 1.0)
    is_qk = j < n_qk_tiles
    for g in range(tc // hd):
        yg = y[:, g * hd:(g + 1) * hd]
        nrm = lax.rsqrt(jnp.sum(yg * yg, axis=1, keepdims=True) + 1e-6) * qscale
        o_ref[:, g * hd:(g + 1) * hd] = yg * jnp.where(is_qk, nrm, 1.0)


def dn_conv(rows, p, conv_w9, dn_w, hd):
    t = rows.t
    tb = rows.tm
    tc = 512
    hb = tb // CONV_HALO
    n_halo = t // CONV_HALO
    kern = functools.partial(_conv_kernel, tb, tc, rows.n_ctx, rows.ctx, rows.seq, 2 * dn_w // tc, dn_w // tc, hd)
    return pl.pallas_call(
        kern,
        grid=(t // tb, 3 * dn_w // tc),
        in_specs=[pl.BlockSpec((CONV_HALO, tc), lambda i, j: (jnp.maximum(i * hb - 1, 0), j)),
                  pl.BlockSpec((tb, tc), lambda i, j: (i, j)),
                  pl.BlockSpec((CONV_HALO, tc), lambda i, j: (jnp.minimum(i * hb + hb, n_halo - 1), j)),
                  pl.BlockSpec((16, tc), lambda i, j: (0, j))],
        out_specs=pl.BlockSpec((tb, tc), lambda i, j: (i, j)),
        out_shape=jax.ShapeDtypeStruct((t, 3 * dn_w), F32),
        scratch_shapes=[pltpu.VMEM((tb + 2 * CONV_HALO, tc), F32), pltpu.VMEM((2, tb + 2 * GRID_W, tc), F32)],
        compiler_params=_cparams(("parallel", "parallel")),
        name="dn_conv",
    )(p, p, p, conv_w9)


def _chunk_masks(direction):
    ri = lax.broadcasted_iota(I32, (LB, LB), 0)
    ci = lax.broadcasted_iota(I32, (LB, LB), 1)
    same = (ri // CHUNK) == (ci // CHUNK)
    rel = ri - ci if direction == 0 else ci - ri
    return same & (rel >= 0), same & (rel > 0), same


def _dn_gate_kernel(nh, ab_ref, alog_ref, dtb_ref, gc_ref, gr_ref, gt_ref):
    x = ab_ref[...]
    lane = lax.broadcasted_iota(I32, x.shape, 1)
    g = -jnp.exp(alog_ref[...]) * _softplus(x + dtb_ref[...])
    g = jnp.where(lane < 2 * nh, g, 0.0)
    incl_f, _, same = _chunk_masks(0)
    incl_b, _, _ = _chunk_masks(1)
    pre = _fdot(incl_f.astype(F32), g)
    suf = _fdot(incl_b.astype(F32), g)
    tot = _fdot(same.astype(F32), g)
    out = jnp.where(lane < nh, pre, jnp.where(lane < 2 * nh, suf, _sigmoid(x)))
    gc_ref[...] = out
    gr_ref[...] = out.T
    gt_ref[...] = tot


def dn_gates(pg, alog_row, dtb_row, nh):
    t = pg.shape[0]
    return pl.pallas_call(
        functools.partial(_dn_gate_kernel, nh),
        grid=(t // LB,),
        in_specs=[pl.BlockSpec((LB, LANES), lambda i: (i, 0)), pl.BlockSpec((1, LANES), lambda i: (0, 0)),
                  pl.BlockSpec((1, LANES), lambda i: (0, 0))],
        out_specs=[pl.BlockSpec((LB, LANES), lambda i: (i, 0)), pl.BlockSpec((None, LANES, LB), lambda i: (i, 0, 0)),
                   pl.BlockSpec((LB, LANES), lambda i: (i, 0))],
        out_shape=[jax.ShapeDtypeStruct((t, LANES), F32), jax.ShapeDtypeStruct((t // LB, LANES, LB), F32),
                   jax.ShapeDtypeStruct((t, LANES), F32)],
        compiler_params=_cparams(("parallel",)),
        name="dn_gates",
    )(pg, alog_row, dtb_row)


DN_HG = 8
DN_SCAN_HG = 8


def _dn_ut_kernel(nh, hd, q_ref, k_ref, v_ref, gc_ref, gr_ref, gt_ref, *refs):
    outs = (refs[0:5], refs[5:10])
    a_ref, t_ref, x_ref, rhs_ref = refs[10:]
    hg = pl.program_id(1)
    ri = lax.broadcasted_iota(I32, (LB, LB), 0)
    ci = lax.broadcasted_iota(I32, (LB, LB), 1)
    eye = (ri == ci).astype(F32)
    same = (ri // CHUNK) == (ci // CHUNK)
    gcb = gc_ref[...]
    gtb = gt_ref[...]
    chains = [(d, hh) for hh in range(DN_HG) for d in (0, 1)]
    for c, (d, hh) in enumerate(chains):
        u_ref, w_ref, qg_ref, kd_ref, att_ref = outs[d]
        rel = ri - ci if d == 0 else ci - ri
        lane_g = d * nh + hg * DN_HG + hh
        gc_col = _lane_col(gcb, lane_g)
        beta = _lane_col(gcb, 2 * nh + lane_g)
        gt_col = _lane_col(gtb, lane_g)
        gc_row = gr_ref[pl.ds(lane_g, 1), :]
        sl = slice(hh * hd, (hh + 1) * hd)
        q, k, v = q_ref[:, sl], k_ref[:, sl], v_ref[:, sl]
        if d == 0:
            kk, qk = _bdot_nt(k, k), _bdot_nt(q, k)
        decay = jnp.exp(jnp.where(same & (rel >= 0), gc_col - gc_row, -jnp.inf))
        kb = k * beta
        a = jnp.where(rel > 0, beta * kk * decay, 0.0)
        a_ref[c] = a
        t_ref[c] = eye - jnp.where((ri // 2) == (ci // 2), a, 0.0)
        eg = jnp.exp(gc_col)
        rhs_ref[c] = jnp.concatenate([v * beta, kb * eg], axis=1).astype(rhs_ref.dtype)
        qg_ref[:, sl] = (q * eg).astype(qg_ref.dtype)
        kd_ref[sl, :] = (k * jnp.exp(gt_col - gc_col)).T.astype(kd_ref.dtype)
        att_ref[:, hh * LB:(hh + 1) * LB] = (qk * decay).astype(att_ref.dtype)
    s = 2
    while s < CHUNK:
        m = ((ri // (2 * s)) == (ci // (2 * s))) & ((ri // s) != (ci // s))
        for c in range(len(chains)):
            x_ref[c] = _bdot(t_ref[c], jnp.where(m, a_ref[c], 0.0)).astype(x_ref.dtype)
        for c in range(len(chains)):
            t = t_ref[c]
            t_ref[c] = t - _bdot(x_ref[c], t)
        s *= 2
    for c, (d, hh) in enumerate(chains):
        u_ref, w_ref = outs[d][0], outs[d][1]
        sl = slice(hh * hd, (hh + 1) * hd)
        sol = _bdot(t_ref[c], rhs_ref[c])
        u_ref[:, sl] = sol[:, :hd]
        w_ref[:, sl] = sol[:, hd:].astype(w_ref.dtype)


def _dn_scan_kernel(direction, nh, hd, hgs, *refs):
    s_refs, vn_refs = refs[-2 * hgs:-hgs], refs[-hgs:]
    if direction == 0:
        u_ref, w_ref, qg_ref, kd_ref, att_ref, gt_ref, o_ref = refs[:-2 * hgs]
    else:
        u_ref, w_ref, qg_ref, kd_ref, att_ref, gt_ref, of_ref, z_ref, ng_ref, o_ref = refs[:-2 * hgs]
    hg = pl.program_id(1)
    step = pl.program_id(2)

    @pl.when(step == 0)
    def _():
        for s_ref in s_refs:
            s_ref[...] = jnp.zeros_like(s_ref)

    for vn_ref in vn_refs:
        vn_ref[...] = jnp.zeros_like(vn_ref)
    gtb = gt_ref[...]
    egs = [jnp.exp(_lane_col(gtb, direction * nh + hg * hgs + hh)) for hh in range(hgs)]
    n_chunks = LB // CHUNK
    for ci in range(n_chunks):
        cc = ci if direction == 0 else n_chunks - 1 - ci
        rs = slice(cc * CHUNK, (cc + 1) * CHUNK)
        sls = [slice(hh * hd, (hh + 1) * hd) for hh in range(hgs)]
        ss = [s_ref[...] for s_ref in s_refs]
        rr = [_bdot(jnp.concatenate([w_ref[rs, sl], qg_ref[rs, sl]], axis=0), s)
              for sl, s in zip(sls, ss)]
        for hh in range(hgs):
            v_new = u_ref[rs, sls[hh]] - rr[hh][:CHUNK]
            vn_refs[hh][rs, :] = v_new
            parts = [jnp.zeros((n, hd), F32) for n in (cc * CHUNK,) if n] + [v_new]
            parts += [jnp.zeros((n, hd), F32) for n in (LB - (cc + 1) * CHUNK,) if n]
            s_refs[hh][...] = (egs[hh][cc * CHUNK:cc * CHUNK + 1, :] * ss[hh]
                               + _bdot(kd_ref[sls[hh], :], jnp.concatenate(parts, axis=0)))
        for hh in range(hgs):
            sl = sls[hh]
            o = rr[hh][CHUNK:] + _bdot(att_ref[rs, hh * LB:(hh + 1) * LB], vn_refs[hh][...])
            if direction == 0:
                o_ref[rs, sl] = o
            else:
                o = o + of_ref[rs, sl]
                rms = lax.rsqrt(jnp.mean(o * o, axis=1, keepdims=True) + 1e-6)
                o_ref[rs, sl] = (o * rms * ng_ref[...] * _silu(z_ref[rs, sl])).astype(o_ref.dtype)


def gated_deltanet(order, qkv, p, gc, gr, gt, norm_g, nh, hd):
    t = qkv.shape[0]
    b = order.b
    w = nh * hd
    gw = DN_HG * hd
    ng = nh // DN_HG
    nchain = 2 * DN_HG

    per_dir_specs = ([pl.BlockSpec((LB, gw), lambda i, g: (i, g))] * 3 + [pl.BlockSpec((gw, LB), lambda i, g: (g, i))]
                     + [pl.BlockSpec((LB, DN_HG * LB), lambda i, g: (i, g))])
    per_dir_shape = ([jax.ShapeDtypeStruct((t, w), F32)] + [jax.ShapeDtypeStruct((t, w), BF16)] * 2
                     + [jax.ShapeDtypeStruct((w, t), BF16), jax.ShapeDtypeStruct((t, nh * LB), BF16)])
    ut = pl.pallas_call(
        functools.partial(_dn_ut_kernel, nh, hd),
        grid=(t // LB, ng),
        in_specs=[pl.BlockSpec((LB, gw), lambda i, g: (i, g)), pl.BlockSpec((LB, gw), lambda i, g: (i, ng + g)),
                  pl.BlockSpec((LB, gw), lambda i, g: (i, 2 * ng + g)),
                  pl.BlockSpec((LB, LANES), lambda i, g: (i, 0)),
                  pl.BlockSpec((None, LANES, LB), lambda i, g: (i, 0, 0)),
                  pl.BlockSpec((LB, LANES), lambda i, g: (i, 0))],
        out_specs=per_dir_specs * 2,
        out_shape=per_dir_shape * 2,
        scratch_shapes=[pltpu.VMEM((nchain, LB, LB), F32), pltpu.VMEM((nchain, LB, LB), F32),
                        pltpu.VMEM((nchain, LB, LB), BF16), pltpu.VMEM((nchain, LB, 2 * hd), BF16)],
        compiler_params=_cparams(("parallel", "parallel")),
        name="dn_ut",
    )(qkv, qkv, qkv, gc, gr, gt)
    outs = (ut[0:5], ut[5:10])

    hgs = min(DN_SCAN_HG, nh)
    sw = hgs * hd
    nsg = nh // hgs

    def run(direction, extra_in, extra_specs, out_dtype):
        def blk(width):
            return pl.BlockSpec((LB, width), lambda bi, g, s: (order.row_block(direction, bi, s), g))
        return pl.pallas_call(
            functools.partial(_dn_scan_kernel, direction, nh, hd, hgs),
            grid=(b, nsg, order.steps),
            in_specs=[blk(sw)] * 3 + [pl.BlockSpec((sw, LB), lambda bi, g, s: (g, order.row_block(direction, bi, s))),
                      blk(hgs * LB),
                      pl.BlockSpec((LB, LANES), lambda bi, g, s: (order.row_block(direction, bi, s), 0))] + extra_specs,
            out_specs=blk(sw),
            out_shape=jax.ShapeDtypeStruct((t, w), out_dtype),
            scratch_shapes=[pltpu.VMEM((hd, hd), F32)] * hgs + [pltpu.VMEM((LB, hd), F32)] * hgs,
            compiler_params=_cparams(("parallel", "parallel", "arbitrary")),
            name=f"dn_scan_d{direction}",
        )(*outs[direction], gt, *extra_in)

    o_fwd = run(0, [], [], F32)
    return run(1, [o_fwd, p, norm_g.reshape(1, hd)],
               [pl.BlockSpec((LB, sw), lambda bi, g, s: (order.row_block(1, bi, s), g)),
                pl.BlockSpec((LB, sw), lambda bi, g, s: (order.row_block(1, bi, s), 3 * nsg + g)),
                pl.BlockSpec((1, hd), lambda bi, g, s: (0, 0))], BF16)


def _pack_pairs(x):
    half = x.shape[1] // 2
    lo = pltpu.bitcast(x[:, :half].astype(BF16).astype(F32), U32) >> 16
    hi = pltpu.bitcast(x[:, half:].astype(BF16).astype(F32), U32) & jnp.uint32(0xFFFF0000)
    return hi | lo


def _unpack_pairs(w):
    lo = pltpu.bitcast(w << 16, F32)
    hi = pltpu.bitcast(w & jnp.uint32(0xFFFF0000), F32)
    return jnp.concatenate([lo, hi], axis=1)


def _ln_pack_kernel(alpha, n_ctx_tiles, *refs):
    if n_ctx_tiles is None:
        h_ref, z_ref, gate_ref, lng_ref, lnb_ref, sc_ref, sh_ref, hn_ref, vp_ref = refs
        h = h_ref[...]
    else:
        c_ref, x_ref, z_ref, gate_ref, lng_ref, lnb_ref, sc_ref, sh_ref, hn_ref, vp_ref = refs
        h = jnp.where(pl.program_id(0) < n_ctx_tiles, c_ref[...], x_ref[...])
    y = _layer_norm_rows(alpha * h + gate_ref[...] * z_ref[...], lng_ref[...], lnb_ref[...])
    hn_ref[...] = y
    vp_ref[...] = _pack_pairs(y * (1.0 + sc_ref[...]) + sh_ref[...])


def ln_residual_pack(rows, h, z, modv, layer, lng, lnb, alpha):
    tm = rows.tm // 2
    d = rows.d
    vec = pl.BlockSpec((1, d), lambda i: (0, 0))
    two_src = isinstance(h, tuple)
    h_specs = rows.src_specs(tm) if two_src else [rows.row_spec(tm)]
    h_args = list(h) if two_src else [h]
    return pl.pallas_call(
        functools.partial(_ln_pack_kernel, alpha, rows.n_ctx // tm if two_src else None),
        grid=(rows.t // tm,),
        in_specs=h_specs + [rows.row_spec(tm), rows.mod_spec(layer, 2, tm), vec, vec,
                            rows.mod_spec(layer, 4, tm), rows.mod_spec(layer, 3, tm)],
        out_specs=[rows.row_spec(tm), rows.row_spec(tm, d // 2)],
        out_shape=[jax.ShapeDtypeStruct((rows.t, d), F32), jax.ShapeDtypeStruct((rows.t, d // 2), U32)],
        compiler_params=_cparams(("parallel",)),
        name="ln_residual_pack",
    )(*h_args, z, modv, lng.reshape(1, d), lnb.reshape(1, d), modv, modv)


def _router_kernel(tm, ne, h_ref, sc_ref, sh_ref, w_ref, bias_ref, eid_ref, pos_ref, wgt_ref, cnt_ref, run_ref):
    i = pl.program_id(0)

    @pl.when(i == 0)
    def _():
        run_ref[...] = jnp.zeros_like(run_ref)

    x = h_ref[...] * (1.0 + sc_ref[...]) + sh_ref[...]
    w = w_ref[...]
    x_hi, w_hi = x.astype(BF16), w.astype(BF16)
    x_lo, w_lo = x - x_hi.astype(F32), w - w_hi.astype(F32)
    scores = _sigmoid(_bdot(x_hi, w_hi) + _bdot(x_hi, w_lo) + _bdot(x_lo, w_hi))
    biased = scores + bias_ref[...]
    lane = lax.broadcasted_iota(I32, (tm, ne), 1)
    gsz = ne // N_GROUPS
    grp = lane // gsz
    neg = -jnp.inf

    gs = []
    for g in range(N_GROUPS):
        m = jnp.where(grp == g, biased, neg)
        top1 = jnp.max(m, axis=1, keepdims=True)
        n_top = jnp.sum((m == top1).astype(F32), axis=1, keepdims=True)
        rest = jnp.max(jnp.where(m < top1, m, neg), axis=1, keepdims=True)
        gs.append(top1 + jnp.where(n_top >= 2.0, top1, rest))
    emask = jnp.zeros((tm, ne), jnp.bool_)
    for g in range(N_GROUPS):
        beaten = jnp.zeros((tm, 1), F32)
        for g2 in range(N_GROUPS):
            if g2 == g:
                continue
            wins = (gs[g2] > gs[g]) | (gs[g2] == gs[g]) if g2 < g else gs[g2] > gs[g]
            beaten = beaten + wins.astype(F32)
        emask = emask | ((grp == g) & (beaten < TOPK_GROUPS))
    masked = jnp.where(emask, biased, neg)

    sel = jnp.zeros((tm, ne), jnp.bool_)
    idxs = []
    lane_f = lane.astype(F32)
    for _ in range(TOP_K):
        mx = jnp.max(masked, axis=1, keepdims=True)
        idx = jnp.min(jnp.where(masked == mx, lane_f, float(ne)), axis=1, keepdims=True).astype(I32)
        hit = lane == idx
        sel = sel | hit
        masked = jnp.where(hit, neg, masked)
        idxs.append(idx)
    w_sel = jnp.where(sel, scores, 0.0)
    w_sel = w_sel / jnp.sum(w_sel, axis=1, keepdims=True) * ROUTED_SCALE

    ri = lax.broadcasted_iota(I32, (tm, tm), 0)
    ci = lax.broadcasted_iota(I32, (tm, tm), 1)
    self = sel.astype(F32)
    before = _bdot((ri > ci).astype(F32), self) + run_ref[0:1, :]
    run_ref[0:1, :] = run_ref[0:1, :] + jnp.sum(self, axis=0, keepdims=True)
    cnt_ref[...] = jnp.broadcast_to(run_ref[0:1, :], cnt_ref.shape)

    lane_o = lax.broadcasted_iota(I32, (tm, LANES), 1)
    eid = jnp.zeros((tm, LANES), I32)
    pos = jnp.zeros((tm, LANES), I32)
    wgt = jnp.zeros((tm, LANES), F32)
    for kk, idx in enumerate(idxs):
        hit = lane == idx
        p_k = jnp.sum(jnp.where(hit, before, 0.0), axis=1, keepdims=True).astype(I32)
        w_k = jnp.sum(jnp.where(hit, w_sel, 0.0), axis=1, keepdims=True)
        eid = jnp.where(lane_o == kk, idx, eid)
        pos = jnp.where(lane_o == kk, p_k, pos)
        wgt = jnp.where(lane_o == kk, w_k, wgt)
    eid_ref[...] = eid
    pos_ref[...] = pos
    wgt_ref[...] = wgt


def router(rows, h, modv, layer, router_w, router_bias):
    t, d = h.shape
    ne = router_w.shape[1]
    tm = rows.tm // 2
    row = pl.BlockSpec((tm, LANES), lambda i: (i, 0))
    return pl.pallas_call(
        functools.partial(_router_kernel, tm, ne),
        grid=(t // tm,),
        in_specs=[rows.row_spec(tm), rows.mod_spec(layer, 4, tm), rows.mod_spec(layer, 3, tm),
                  pl.BlockSpec((d, ne), lambda i: (0, 0)), pl.BlockSpec((1, ne), lambda i: (0, 0))],
        out_specs=[row, row, row, pl.BlockSpec((8, ne), lambda i: (0, 0))],
        out_shape=[jax.ShapeDtypeStruct((t, LANES), I32), jax.ShapeDtypeStruct((t, LANES), I32),
                   jax.ShapeDtypeStruct((t, LANES), F32), jax.ShapeDtypeStruct((8, ne), F32)],
        scratch_shapes=[pltpu.VMEM((8, ne), F32)],
        compiler_params=_cparams(("arbitrary",)),
        name="router",
    )(h, modv, modv, router_w, router_bias.reshape(1, ne))


def _dispatch_kernel(tm, ne, n_all_tiles, pstart_ref, npad_ref, nt_ref, dest_ref, x_ref, xs_ref, zero_ref, sem, sem_pad):
    def pad_copy(e, r):
        return pltpu.make_async_copy(zero_ref.at[pl.ds(0, 1), :], xs_ref.at[pl.ds(pstart_ref[e] + r, 1), :], sem_pad)

    def tail_copy(j):
        return pltpu.make_async_copy(zero_ref, xs_ref.at[pl.ds(pl.multiple_of(j * TME, TME), TME), :], sem_pad)

    @pl.when(pl.program_id(0) == 0)
    def _():
        zero_ref[...] = jnp.zeros_like(zero_ref)

        def fill(e, c):
            lax.fori_loop(0, npad_ref[e], lambda r, c2: (pad_copy(e, r).start(), c2)[1], 0)
            return c

        lax.fori_loop(0, ne, fill, 0)
        lax.fori_loop(nt_ref[0], n_all_tiles, lambda j, c: (tail_copy(j).start(), c)[1], 0)

        def drain(e, c):
            lax.fori_loop(0, npad_ref[e], lambda r, c2: (pad_copy(e, r).wait(), c2)[1], 0)
            return c

        lax.fori_loop(0, ne, drain, 0)
        lax.fori_loop(nt_ref[0], n_all_tiles, lambda j, c: (tail_copy(j).wait(), c)[1], 0)

    def issue(t, c):
        for kk in range(TOP_K):
            pltpu.make_async_copy(x_ref.at[pl.ds(t, 1), :], xs_ref.at[pl.ds(dest_ref[0, t * TOP_K + kk], 1), :], sem).start()
        return c

    lax.fori_loop(0, tm, issue, 0)
    for _ in range(TOP_K):
        pltpu.make_async_copy(x_ref, x_ref, sem).wait()


def dispatch(vp, dest, pad_start, n_pad, n_tiles, n_rows):
    t, dh = vp.shape
    ne = pad_start.shape[0]
    tm = _pick_tile(t, (256, 128))
    gs = pltpu.PrefetchScalarGridSpec(
        num_scalar_prefetch=3,
        grid=(t // tm,),
        in_specs=[pl.BlockSpec((None, 1, tm * TOP_K), lambda i, ps, npd, nt: (i, 0, 0), memory_space=pltpu.SMEM),
                  pl.BlockSpec((tm, dh), lambda i, ps, npd, nt: (i, 0))],
        out_specs=pl.BlockSpec(memory_space=pl.ANY),
        scratch_shapes=[pltpu.VMEM((TME, dh), U32), pltpu.SemaphoreType.DMA, pltpu.SemaphoreType.DMA],
    )
    return pl.pallas_call(
        functools.partial(_dispatch_kernel, tm, ne, n_rows // TME),
        grid_spec=gs,
        out_shape=jax.ShapeDtypeStruct((n_rows, dh), U32),
        compiler_params=_cparams(("arbitrary",)),
        name="moe_dispatch",
    )(pad_start, n_pad, n_tiles, dest.reshape(t // tm, 1, tm * TOP_K), vp)


def _swiglu_kernel(eh, pack_out, te_ref, nt_ref, x_ref, wi_ref, wo_ref, o_ref, wib_ref, wob_ref):
    j = pl.program_id(0)

    @pl.when(j < nt_ref[0])
    def _():
        @pl.when((j == 0) | (te_ref[j] != te_ref[jnp.maximum(j - 1, 0)]))
        def _():
            wib_ref[...] = wi_ref[...].astype(wib_ref.dtype)
            wob_ref[...] = wo_ref[...].astype(wob_ref.dtype)

        ab = _bdot(_unpack_pairs(x_ref[...]), wib_ref[...])
        act = _silu(ab[:, :eh]) * ab[:, eh:]
        y = _bdot(act, wob_ref[...])
        o_ref[...] = _pack_pairs(y) if pack_out else y

    @pl.when(j >= nt_ref[0])
    def _():
        o_ref[...] = jnp.zeros_like(o_ref)


def grouped_swiglu(xs, w_in, w_out, layer, tile_expert, n_tiles, tile, pack_out):
    r, dh = xs.shape
    _, ne, d, eh2 = w_in.shape
    eh = eh2 // 2
    grid_n = r // tile
    out_w, out_dtype = (dh, U32) if pack_out else (d, F32)

    def row_map(j, te, nt):
        return (jnp.minimum(j, nt[0] - 1), 0)

    gs = pltpu.PrefetchScalarGridSpec(
        num_scalar_prefetch=2,
        grid=(grid_n,),
        in_specs=[pl.BlockSpec((tile, dh), row_map),
                  pl.BlockSpec((None, None, d, eh2), lambda j, te, nt: (layer, te[j], 0, 0)),
                  pl.BlockSpec((None, None, eh, d), lambda j, te, nt: (layer, te[j], 0, 0))],
        out_specs=pl.BlockSpec((tile, out_w), lambda j, te, nt: (j, 0)),
        scratch_shapes=[pltpu.VMEM((d, eh2), BF16), pltpu.VMEM((eh, d), BF16)],
    )
    return pl.pallas_call(
        functools.partial(_swiglu_kernel, eh, pack_out),
        grid_spec=gs,
        out_shape=jax.ShapeDtypeStruct((r, out_w), out_dtype),
        compiler_params=_cparams(("arbitrary",)),
        name="grouped_swiglu",
    )(tile_expert, n_tiles, xs, w_in, w_out)


def _combine_kernel(tm, alpha, has_next, dest_ref, destn_ref, wgt_ref, sh_ref, h_ref, gate_ref, lng_ref, lnb_ref,
                    sc_ref, sh2_ref, ys_ref, *rest):
    if has_next:
        hn_ref, u_ref, buf_ref, sem = rest
    else:
        hn_ref, buf_ref, sem = rest
    i = pl.program_id(0)
    slot = i % 2

    def gather(d_ref, s):
        def body(t, c):
            for kk in range(TOP_K):
                pltpu.make_async_copy(ys_ref.at[pl.ds(d_ref[0, t * TOP_K + kk], 1), :],
                                      buf_ref.at[s, kk, pl.ds(t, 1), :], sem.at[s]).start()
            return c
        lax.fori_loop(0, tm, body, 0)

    @pl.when(i == 0)
    def _():
        gather(dest_ref, 0)

    @pl.when(i + 1 < pl.num_programs(0))
    def _():
        gather(destn_ref, 1 - slot)

    pltpu.make_async_copy(buf_ref.at[slot], buf_ref.at[slot], sem.at[slot]).wait()

    wgt = wgt_ref[...]
    f = sh_ref[...]
    for kk in range(TOP_K):
        f = f + _lane_col(wgt, kk) * _unpack_pairs(buf_ref[slot, kk])
    y = _layer_norm_rows(alpha * h_ref[...] + gate_ref[...] * f, lng_ref[...], lnb_ref[...])
    hn_ref[...] = y
    if has_next:
        u_ref[...] = (y * (1.0 + sc_ref[...]) + sh2_ref[...]).astype(u_ref.dtype)


def moe_combine_ln(rows, ys, dest, wgt, shared, h, modv, layer, lng, lnb, alpha, has_next):
    tm = 128
    t, d = rows.t, rows.d
    dh = ys.shape[1]
    n_all = t // tm
    off = 0 if has_next else rows.n_ctx // tm
    n_out = n_all - off
    vec = pl.BlockSpec((1, d), lambda i: (0, 0))
    nl = layer + 1 if has_next else layer

    def row(width):
        return pl.BlockSpec((tm, width), lambda i: (i + off, 0))

    def mod(lyr, which):
        return pl.BlockSpec((None, 1, d), lambda i: ((lyr * MOD_ROWS + rows.seg(i + off, tm)) * 6 + which, 0, 0))

    out_specs = [pl.BlockSpec((tm, d), lambda i: (i, 0))]
    out_shape = [jax.ShapeDtypeStruct((n_out * tm, d), F32)]
    if has_next:
        out_specs.append(pl.BlockSpec((tm, d), lambda i: (i, 0)))
        out_shape.append(jax.ShapeDtypeStruct((n_out * tm, d), BF16))
    dest3 = dest.reshape(n_all, 1, tm * TOP_K)
    return pl.pallas_call(
        functools.partial(_combine_kernel, tm, alpha, has_next),
        grid=(n_out,),
        in_specs=[pl.BlockSpec((None, 1, tm * TOP_K), lambda i: (i + off, 0, 0), memory_space=pltpu.SMEM),
                  pl.BlockSpec((None, 1, tm * TOP_K), lambda i: (jnp.minimum(i + off + 1, n_all - 1), 0, 0),
                               memory_space=pltpu.SMEM),
                  row(LANES), row(d), row(d), mod(layer, 5), vec, vec, mod(nl, 1), mod(nl, 0),
                  pl.BlockSpec(memory_space=pl.ANY)],
        out_specs=out_specs,
        out_shape=out_shape,
        scratch_shapes=[pltpu.VMEM((2, TOP_K, tm, dh), U32), pltpu.SemaphoreType.DMA((2,))],
        compiler_params=_cparams(("arbitrary",)),
        name="moe_combine_ln",
    )(dest3, dest3, wgt, shared, h, modv, lng.reshape(1, d), lnb.reshape(1, d), modv, modv, ys)


def moe_ffn(rows, vp, h, modv, layer, lng, lnb, alpha, has_next, router_w, router_bias, exp_w_in, exp_w_out,
            sh_w_in, sh_w_out):
    t = vp.shape[0]
    ne = router_w.shape[1]
    eid, pos, wgt, cnt = router(rows, h, modv, layer, router_w, router_bias)
    counts = cnt[0].astype(I32)
    padded = (counts + TME - 1) // TME * TME
    ends = jnp.cumsum(padded)
    offs = ends - padded
    dest = offs[eid[:, :TOP_K]] + pos[:, :TOP_K]
    n_rows = (t * TOP_K // TME + ne) * TME
    tile_start = jnp.arange(n_rows // TME, dtype=I32) * TME
    tile_expert = jnp.minimum(jnp.sum((tile_start[:, None] >= ends[None, :]).astype(I32), axis=1), ne - 1)
    n_tiles = (ends[-1] // TME).reshape(1)

    xs = dispatch(vp, dest, offs + counts, padded - counts, n_tiles, n_rows)
    ys = grouped_swiglu(xs, exp_w_in, exp_w_out, layer, tile_expert, n_tiles, TME, True)
    st = rows.tm // 2
    shared = grouped_swiglu(vp, sh_w_in[:, None], sh_w_out[:, None], layer, jnp.zeros((t // st,), I32),
                            jnp.full((1,), t // st, I32), st, False)
    return moe_combine_ln(rows, ys, dest, wgt, shared, h, modv, layer, lng, lnb, alpha, has_next)


def _pad_cols(w, n):
    return jnp.pad(w, ((0, 0), (0, n - w.shape[1])))


def _pad_row(v, n=LANES):
    v = v.reshape(1, -1).astype(F32)
    return jnp.pad(v, ((0, 0), (0, n - v.shape[1])))


def retention_mlstm_mixer(rows, order, u, w_in, w_out, ret_decay_rate, ret_norm_g, ml_gate_b, ml_norm_g):
    nh_r = ret_decay_rate.shape[-1]
    nh_m = ml_gate_b.shape[-1]
    ret_w = ret_norm_g.shape[0]
    ml_w = ml_norm_g.shape[0]
    hd_r, hd_m = ret_w // nh_r, ml_w // nh_m
    assert hd_r == hd_m
    main_w = 4 * ret_w + 4 * ml_w
    wb = w_in.astype(BF16)
    p = matmul(u, wb[:, :main_w], name="ab_in_proj")
    pg = matmul(u, _pad_cols(wb[:, main_w:], LANES), name="ab_gate_proj")

    pos = jnp.arange(rows.ctx + rows.seq, dtype=F32)
    half = hd_r // 2
    inv_freq = ROPE_BASE ** (-jnp.arange(half, dtype=F32) / half)
    ang = pos[:, None] * inv_freq[None, :]
    rate = jnp.broadcast_to(ret_decay_rate.astype(F32).reshape(2 * nh_r, 1, 1), (2 * nh_r, 1, LANES))
    y_r = retention(order, p, jnp.cos(ang), jnp.sin(ang), rate, ret_norm_g, nh_r, hd_r)

    gc, gr = ml_gates(pg, _pad_row(ml_gate_b), nh_m)
    y_m = mlstm(order, p, 4 * nh_r, gc, gr, ml_norm_g, nh_m, hd_m)
    return matmul2(y_r, y_m, w_out.astype(BF16), name="ab_out_proj")


def gated_deltanet_mixer(rows, order, u, w_in, conv_w, a_log, dt_bias, norm_g, w_out):
    nh = a_log.shape[-1]
    hd = norm_g.shape[0]
    dn_w = nh * hd
    wb = w_in.astype(BF16)
    p = matmul(u, wb[:, :4 * dn_w], name="dn_in_proj")
    pg = matmul(u, _pad_cols(wb[:, 4 * dn_w:], LANES), name="dn_gate_proj")
    conv_w9 = jnp.pad(conv_w.reshape(9, 3 * dn_w).astype(F32), ((0, 7), (0, 0)))
    qkv = dn_conv(rows, p, conv_w9, dn_w, hd)
    gc, gr, gt = dn_gates(pg, _pad_row(a_log), _pad_row(dt_bias), nh)
    y = gated_deltanet(order, qkv, p, gc, gr, gt, norm_g, nh, hd)
    return matmul(y, w_out.astype(BF16), name="dn_out_proj")


def kernel(x, c, ctx, c_ctx, ada_w, ada_b, ln_g, ln_b, ab_w_in, ab_w_out, ret_decay_rate, ret_norm_g, ml_gate_b, ml_norm_g, dn_w_in, dn_conv_w, dn_a_log, dn_dt_bias, dn_norm_g, dn_w_out, router_w, router_bias, exp_w_in, exp_w_out, sh_w_in, sh_w_out):
    b, seq, d = x.shape
    n_ctx = ctx.shape[1]
    depth = ada_w.shape[0]
    alpha = (2 * depth) ** 0.25
    rows = Rows(b, n_ctx, seq, d)
    order = ScanOrder(b, n_ctx, seq)
    assert b + 1 <= MOD_ROWS

    cvec = jnp.zeros((MOD_ROWS, d), F32).at[:b].set(c).at[b].set(c_ctx)
    modv = ada_table(cvec, ada_w, ada_b).reshape(depth * MOD_ROWS * 6, 1, d)

    h = (ctx.reshape(b * n_ctx, d), x.reshape(b * seq, d))
    u = modulate(rows, h[0], h[1], modv, 0, 1, 0)
    for layer in range(depth):
        j = layer // 2
        if layer % 2 == 0:
            z = retention_mlstm_mixer(rows, order, u, ab_w_in[j], ab_w_out[j], ret_decay_rate[j], ret_norm_g[j],
                                      ml_gate_b[j], ml_norm_g[j])
        else:
            z = gated_deltanet_mixer(rows, order, u, dn_w_in[j], dn_conv_w[j], dn_a_log[j], dn_dt_bias[j],
                                     dn_norm_g[j], dn_w_out[j])
        h, vp = ln_residual_pack(rows, h, z, modv, layer, ln_g[layer, 0], ln_b[layer, 0], alpha)
        has_next = layer < depth - 1
        outs = moe_ffn(rows, vp, h, modv, layer, ln_g[layer, 1], ln_b[layer, 1], alpha, has_next, router_w[layer],
                       router_bias[layer], exp_w_in, exp_w_out, sh_w_in, sh_w_out)
        if has_next:
            h, u = outs
        else:
            h = outs[0]
    return h.reshape(b, seq, d)
```
